```python
import math
import jax
import jax.numpy as jnp
from jax import lax
import numpy as np

D_MODEL = 2048
BATCH = 2
SEQ = 4096
DEPTH = 4
DEC_BATCH = 8
DEC_SEQ = 4
PAST_LEN = 16384
PAGE_SIZE = 128

ATTN_PATTERNS = ((128, 1), (512, 4), (2048, 16))
N_ATTN_GROUPS = len(ATTN_PATTERNS)
HEADS_PER_GROUP = 4
HEAD_DIM = 128
N_ATTN_HEADS = N_ATTN_GROUPS * HEADS_PER_GROUP
ATTN_WIDTH = N_ATTN_HEADS * HEAD_DIM
ATTN_OUT = HEADS_PER_GROUP * HEAD_DIM
ALIBI_SLOPES = tuple(2.0 ** (-8.0 * (h + 1) / N_ATTN_HEADS) for h in range(N_ATTN_HEADS))
BAND_BLOCK = 128

SSM_GROUP = 16
SSM_WIDTH = 3 * D_MODEL // 4
N_SSM_GROUPS = SSM_WIDTH // SSM_GROUP
SSM_STATE = 64
DT_MIN = 1e-3
DT_MAX = 1e-1

IN_SPLITS = (ATTN_WIDTH, 2 * ATTN_WIDTH, 3 * ATTN_WIDTH, 3 * ATTN_WIDTH + SSM_WIDTH,
             3 * ATTN_WIDTH + SSM_WIDTH + D_MODEL)
IN_COLS = 3 * ATTN_WIDTH + SSM_WIDTH + 2 * D_MODEL

N_EXPERTS = 32
TOP_K = 4
D_FF = D_MODEL
SWIGLU_LIMIT = 7.0
SWIGLU_ALPHA = 1.702
MOE_BLOCK_MAX = 128
MOE_BLOCK_MIN = 8

DN_ALPHA = (2 * DEPTH) ** 0.25
DN_BETA = (8 * DEPTH) ** -0.25
LN_EPS = 1e-5

kernel_name = 'hybrid_dilated_attn_s5_moe_decoder_step'


def layer_norm(x, g, b):
    xf = x.astype(jnp.float32)
    mu = xf.mean(-1, keepdims=True)
    var = jnp.square(xf - mu).mean(-1, keepdims=True)
    return ((xf - mu) * lax.rsqrt(var + LN_EPS) * g + b).astype(x.dtype)


def dilated_attention_prompt(q, k, v, window, dilation, slopes):
    bsz, seq, heads, hd = q.shape
    res_len = seq // dilation
    span = window // dilation
    blk = math.gcd(res_len, BAND_BLOCK)
    n_blk = res_len // blk

    def to_residue(a):
        return a.reshape(bsz, res_len, dilation, heads, hd).transpose(0, 2, 3, 1, 4)

    qr = to_residue(q).reshape(bsz, dilation, heads, n_blk, blk, hd)
    pad = ((0, 0), (0, 0), (0, 0), (span, 0), (0, 0))
    kr = jnp.pad(to_residue(k), pad)
    vr = jnp.pad(to_residue(v), pad)
    key_idx = (jnp.arange(n_blk) * blk)[:, None] + jnp.arange(blk + span)[None, :]
    kb = jnp.take(kr, key_idx, axis=3)
    vb = jnp.take(vr, key_idx, axis=3)
    s = jnp.einsum('brhnqc,brhnkc->brhnqk', qr, kb,
                   preferred_element_type=jnp.float32) * (HEAD_DIM ** -0.5)
    q_pos = (jnp.arange(n_blk) * blk)[:, None] + jnp.arange(blk)[None, :]
    k_pos = key_idx - span
    rel = q_pos[:, :, None] - k_pos[:, None, :]
    valid = (rel >= 0) & (rel <= span) & (k_pos[:, None, :] >= 0)
    slope = jnp.asarray(slopes, jnp.float32)[:, None, None, None]
    s = jnp.where(valid, s - slope * (rel * dilation).astype(jnp.float32), -jnp.inf)
    m = s.max(-1)
    p = jnp.exp(s - m[..., None])
    l = p.sum(-1)
    o = jnp.einsum('brhnqk,brhnkc->brhnqc', p, vb.astype(jnp.float32)) / l[..., None]
    o = o.reshape(bsz, dilation, heads, res_len, hd).transpose(0, 3, 1, 2, 4).reshape(bsz, seq, heads, hd)

    def stat_back(a):
        return a.reshape(bsz, dilation, heads, res_len).transpose(0, 3, 1, 2).reshape(bsz, seq, heads)

    return o, stat_back(m), stat_back(l)


def dilated_attention_sample(q, k_all, v_all, window, dilation, slopes):
    n_new = q.shape[1]
    n_all = k_all.shape[1]
    n_past = n_all - n_new
    dist = jnp.arange(window // dilation + 1) * dilation
    idx = n_past + jnp.arange(n_new)[:, None] - dist[None, :]
    valid = idx >= 0
    idx = jnp.clip(idx, 0, n_all - 1)
    kg = jnp.take(k_all, idx, axis=1)
    vg = jnp.take(v_all, idx, axis=1)
    s = jnp.einsum('bshc,bskhc->bhsk', q, kg,
                   preferred_element_type=jnp.float32) * (HEAD_DIM ** -0.5)
    slope = jnp.asarray(slopes, jnp.float32)[:, None, None]
    s = jnp.where(valid, s - slope * dist.astype(jnp.float32), -jnp.inf)
    m = s.max(-1)
    p = jnp.exp(s - m[..., None])
    l = p.sum(-1)
    o = jnp.einsum('bhsk,bskhc->bshc', p, vg.astype(jnp.float32)) / l.transpose(0, 2, 1)[..., None]
    return o, m.transpose(0, 2, 1), l.transpose(0, 2, 1)


def merge_by_denominator(parts):
    o = jnp.stack([pt[0] for pt in parts])
    m = jnp.stack([pt[1] for pt in parts])
    l = jnp.stack([pt[2] for pt in parts])
    w = l * jnp.exp(m - m.max(0, keepdims=True))
    return (w[..., None] * o).sum(0) / w.sum(0)[..., None]


def complex_linear_combine(e1, e2):
    a1r, a1i, b1r, b1i = e1
    a2r, a2i, b2r, b2i = e2
    return (a1r * a2r - a1i * a2i,
            a1r * a2i + a1i * a2r,
            a2r * b1r - a2i * b1i + b2r,
            a2r * b1i + a2i * b1r + b2i)


def s5_branch(u, lam_re, lam_im, log_dt, b_re, b_im, c_re, c_im, d_skip, w_glu, b_glu, h0):
    f32 = jnp.float32
    bsz, seq, _ = u.shape
    uf = u.astype(f32)
    lam_re = lam_re.astype(f32)
    lam_im = lam_im.astype(f32)
    dt = jnp.exp(log_dt.astype(f32))[:, None]
    decay = jnp.exp(lam_re * dt)
    a_re = decay * jnp.cos(lam_im * dt)
    a_im = decay * jnp.sin(lam_im * dt)
    den = lam_re * lam_re + lam_im * lam_im
    f_re = ((a_re - 1.0) * lam_re + a_im * lam_im) / den
    f_im = (a_im * lam_re - (a_re - 1.0) * lam_im) / den
    b_re = b_re.astype(f32)
    b_im = b_im.astype(f32)
    bb_re = f_re[..., None] * b_re - f_im[..., None] * b_im
    bb_im = f_re[..., None] * b_im + f_im[..., None] * b_re
    ug = uf.reshape(bsz, seq, N_SSM_GROUPS, SSM_GROUP)
    x_re = jnp.einsum('btgc,gnc->btgn', ug, bb_re)
    x_im = jnp.einsum('btgc,gnc->btgn', ug, bb_im)
    if h0 is not None:
        h0_re = h0[..., 0].astype(f32)
        h0_im = h0[..., 1].astype(f32)
        x_re = x_re.at[:, 0].add(a_re * h0_re - a_im * h0_im)
        x_im = x_im.at[:, 0].add(a_re * h0_im + a_im * h0_re)
    a_re_t = jnp.broadcast_to(a_re, (1, seq) + a_re.shape)
    a_im_t = jnp.broadcast_to(a_im, (1, seq) + a_im.shape)
    _, _, h_re, h_im = lax.associative_scan(complex_linear_combine, (a_re_t, a_im_t, x_re, x_im), axis=1)
    y = (jnp.einsum('btgn,gcn->btgc', h_re, c_re.astype(f32))
         - jnp.einsum('btgn,gcn->btgc', h_im, c_im.astype(f32)))
    y = y.reshape(bsz, seq, SSM_WIDTH) + d_skip.astype(f32) * uf
    y = jax.nn.gelu(y)
    y = y * jax.nn.sigmoid(jnp.dot(y, w_glu.astype(f32)) + b_glu.astype(f32))
    h_last = jnp.stack([h_re[:, -1], h_im[:, -1]], axis=-1)
    return y.astype(u.dtype), h_last


def token_mixer(x, mix_w, kv_bufs, h0):
    (w_in, w_attn_proj, lam_re, lam_im, log_dt, b_re, b_im, c_re, c_im,
     d_skip, w_glu, b_glu, w_ssm_proj, w_o) = mix_w
    bsz, seq, _ = x.shape
    q, k, v, u, g_attn, g_ssm = jnp.split(jnp.dot(x, w_in), IN_SPLITS, axis=-1)
    head_shape = (bsz, seq, N_ATTN_GROUPS, HEADS_PER_GROUP, HEAD_DIM)
    q, k, v = q.reshape(head_shape), k.reshape(head_shape), v.reshape(head_shape)
    parts, new_kv = [], []
    for g, (window, dilation) in enumerate(ATTN_PATTERNS):
        slopes = ALIBI_SLOPES[g * HEADS_PER_GROUP:(g + 1) * HEADS_PER_GROUP]
        kv_new = jnp.stack([k[:, :, g], v[:, :, g]], axis=2)
        if kv_bufs is None:
            parts.append(dilated_attention_prompt(q[:, :, g], k[:, :, g], v[:, :, g], window, dilation, slopes))
            new_kv.append(kv_new[:, -min(window, seq):])
        else:
            kv_all = jnp.concatenate([kv_bufs[g].astype(kv_new.dtype), kv_new], axis=1)
            parts.append(dilated_attention_sample(q[:, :, g], kv_all[:, :, 0], kv_all[:, :, 1], window, dilation, slopes))
            new_kv.append(kv_all[:, -kv_bufs[g].shape[1]:])
    attn = merge_by_denominator(parts).reshape(bsz, seq, ATTN_OUT).astype(x.dtype)
    ssm, h_last = s5_branch(u, lam_re, lam_im, log_dt, b_re, b_im, c_re, c_im, d_skip, w_glu, b_glu, h0)
    merged = (jax.nn.sigmoid(g_attn) * jnp.dot(attn, w_attn_proj)
              + jax.nn.sigmoid(g_ssm) * jnp.dot(ssm, w_ssm_proj))
    return jnp.dot(merged, w_o), tuple(new_kv), h_last


def moe_ffn(x, router_w, router_b, w_gate_up, b_gate_up, w_down, b_down):
    shp = x.shape
    xt = x.reshape(-1, D_MODEL)
    n_tok = xt.shape[0]
    logits = jnp.dot(xt, router_w, preferred_element_type=jnp.float32) + router_b.astype(jnp.float32)
    top_val, top_idx = lax.top_k(logits, TOP_K)
    gate = jax.nn.softmax(top_val, axis=-1)
    n_assign = n_tok * TOP_K
    blk = max(MOE_BLOCK_MIN, min(MOE_BLOCK_MAX, n_assign // N_EXPERTS))
    n_slots = -(-(n_assign + N_EXPERTS * (blk - 1)) // blk) * blk
    n_blocks = n_slots // blk
    flat_e = top_idx.reshape(-1)
    flat_t = jnp.repeat(jnp.arange(n_tok, dtype=jnp.int32), TOP_K)
    flat_g = gate.reshape(-1)
    order = jnp.argsort(flat_e)
    se, st, sg = flat_e[order], flat_t[order], flat_g[order]
    counts = jnp.bincount(flat_e, length=N_EXPERTS)
    padded = (counts + blk - 1) // blk * blk
    grp_start = jnp.cumsum(counts) - counts
    pad_end = jnp.cumsum(padded)
    pad_start = pad_end - padded
    dest = pad_start[se] + jnp.arange(n_assign) - grp_start[se]
    slot_tok = jnp.zeros((n_slots,), jnp.int32).at[dest].set(st)
    slot_gate = jnp.zeros((n_slots,), jnp.float32).at[dest].set(sg)
    blk_expert = jnp.minimum(jnp.searchsorted(pad_end, jnp.arange(n_blocks) * blk, side='right'), N_EXPERTS - 1)
    xb = xt[slot_tok].reshape(n_blocks, blk, D_MODEL)

    def expert_block(args):
        xe, e = args
        h = jnp.dot(xe, w_gate_up[e]) + b_gate_up[e]
        glu = jnp.minimum(h[:, :D_FF], SWIGLU_LIMIT)
        lin = jnp.clip(h[:, D_FF:], -SWIGLU_LIMIT, SWIGLU_LIMIT)
        act = glu * jax.nn.sigmoid(SWIGLU_ALPHA * glu) * (lin + 1.0)
        return jnp.dot(act, w_down[e]) + b_down[e]

    yb = lax.map(expert_block, (xb, blk_expert)).reshape(n_slots, D_MODEL)
    y = jnp.zeros((n_tok, D_MODEL), jnp.float32).at[slot_tok].add(yb.astype(jnp.float32) * slot_gate[:, None])
    return y.astype(x.dtype).reshape(shp)


def decoder_layer(x, mix_w, ffn_w, norm_w, kv_bufs, h0):
    ln1_g, ln1_b, ln2_g, ln2_b = norm_w
    mixed, new_kv, h_last = token_mixer(x, mix_w, kv_bufs, h0)
    x = layer_norm(DN_ALPHA * x + mixed, ln1_g, ln1_b)
    x = layer_norm(DN_ALPHA * x + moe_ffn(x, *ffn_w), ln2_g, ln2_b)
    return x, new_kv, h_last


def setup_inputs(seed: int = 0) -> dict:
    key = jax.random.key(seed)
    keys = iter(jax.random.split(key, 40))
    f32 = jnp.float32

    def normal(shape, scale):
        return jax.random.normal(next(keys), shape, f32) * scale

    inputs = {}
    inputs['x_prompt'] = normal((BATCH, SEQ, D_MODEL), 1.0)
    inputs['x_sample'] = normal((DEC_BATCH, DEC_SEQ, D_MODEL), 1.0)
    for window, _ in ATTN_PATTERNS:
        inputs['cache_kv_w%d' % window] = normal(
            (DEPTH, DEC_BATCH, min(window, PAST_LEN), 2, HEADS_PER_GROUP, HEAD_DIM), 1.0)
    inputs['state_ssm'] = normal((DEPTH, DEC_BATCH, N_SSM_GROUPS, SSM_STATE, 2), 0.1)
    col_scale = jnp.concatenate([jnp.ones((2 * ATTN_WIDTH,), f32),
                                 jnp.full((ATTN_WIDTH,), DN_BETA, f32),
                                 jnp.ones((SSM_WIDTH + 2 * D_MODEL,), f32)])
    inputs['w_in'] = normal((DEPTH, D_MODEL, IN_COLS), D_MODEL ** -0.5) * col_scale
    inputs['w_attn_proj'] = normal((DEPTH, ATTN_OUT, D_MODEL), ATTN_OUT ** -0.5)
    inputs['ssm_lambda_re'] = -0.5 + normal((DEPTH, N_SSM_GROUPS, SSM_STATE), 0.01)
    inputs['ssm_lambda_im'] = (jnp.pi * jnp.arange(SSM_STATE, dtype=f32)
                               * (1.0 + normal((DEPTH, N_SSM_GROUPS, SSM_STATE), 0.01)))
    inputs['ssm_log_dt'] = jax.random.uniform(next(keys), (DEPTH, N_SSM_GROUPS), f32,
                                              math.log(DT_MIN), math.log(DT_MAX))
    inputs['ssm_b_re'] = normal((DEPTH, N_SSM_GROUPS, SSM_STATE, SSM_GROUP), (2 * SSM_GROUP) ** -0.5)
    inputs['ssm_b_im'] = normal((DEPTH, N_SSM_GROUPS, SSM_STATE, SSM_GROUP), (2 * SSM_GROUP) ** -0.5)
    inputs['ssm_c_re'] = normal((DEPTH, N_SSM_GROUPS, SSM_GROUP, SSM_STATE), (2 * SSM_STATE) ** -0.5)
    inputs['ssm_c_im'] = normal((DEPTH, N_SSM_GROUPS, SSM_GROUP, SSM_STATE), (2 * SSM_STATE) ** -0.5)
    inputs['ssm_d'] = normal((DEPTH, SSM_WIDTH), 1.0)
    inputs['w_glu'] = normal((DEPTH, SSM_WIDTH, SSM_WIDTH), SSM_WIDTH ** -0.5)
    inputs['b_glu'] = normal((DEPTH, SSM_WIDTH), 0.01)
    inputs['w_ssm_proj'] = normal((DEPTH, SSM_WIDTH, D_MODEL), SSM_WIDTH ** -0.5)
    inputs['w_o'] = normal((DEPTH, D_MODEL, D_MODEL), DN_BETA * D_MODEL ** -0.5)
    inputs['ln1_g'] = 1.0 + normal((DEPTH, D_MODEL), 0.01)
    inputs['ln1_b'] = normal((DEPTH, D_MODEL), 0.01)
    inputs['router_w'] = normal((DEPTH, D_MODEL, N_EXPERTS), D_MODEL ** -0.5)
    inputs['router_b'] = normal((DEPTH, N_EXPERTS), 0.01)
    inputs['w_gate_up'] = normal((DEPTH, N_EXPERTS, D_MODEL, 2 * D_FF), DN_BETA * D_MODEL ** -0.5)
    inputs['b_gate_up'] = normal((DEPTH, N_EXPERTS, 2 * D_FF), 0.01)
    inputs['w_down'] = normal((DEPTH, N_EXPERTS, D_FF, D_MODEL), DN_BETA * D_FF ** -0.5)
    inputs['b_down'] = normal((DEPTH, N_EXPERTS, D_MODEL), 0.01)
    inputs['ln2_g'] = 1.0 + normal((DEPTH, D_MODEL), 0.01)
    inputs['ln2_b'] = normal((DEPTH, D_MODEL), 0.01)
    return inputs


def reference(x_prompt, x_sample, cache_kv_w128, cache_kv_w512, cache_kv_w2048, state_ssm,
              w_in, w_attn_proj, ssm_lambda_re, ssm_lambda_im, ssm_log_dt, ssm_b_re, ssm_b_im,
              ssm_c_re, ssm_c_im, ssm_d, w_glu, b_glu, w_ssm_proj, w_o, ln1_g, ln1_b,
              router_w, router_b, w_gate_up, b_gate_up, w_down, b_down, ln2_g, ln2_b):
    caches = (cache_kv_w128, cache_kv_w512, cache_kv_w2048)
    y_prompt, y_sample = x_prompt, x_sample
    kv_p = [[] for _ in ATTN_PATTERNS]
    kv_s = [[] for _ in ATTN_PATTERNS]
    h_p, h_s = [], []
    for layer in range(DEPTH):
        mix_w = (w_in[layer], w_attn_proj[layer], ssm_lambda_re[layer], ssm_lambda_im[layer],
                 ssm_log_dt[layer], ssm_b_re[layer], ssm_b_im[layer], ssm_c_re[layer],
                 ssm_c_im[layer], ssm_d[layer], w_glu[layer], b_glu[layer],
                 w_ssm_proj[layer], w_o[layer])
        ffn_w = (router_w[layer], router_b[layer], w_gate_up[layer], b_gate_up[layer],
                 w_down[layer], b_down[layer])
        norm_w = (ln1_g[layer], ln1_b[layer], ln2_g[layer], ln2_b[layer])
        y_prompt, kvs, h = decoder_layer(y_prompt, mix_w, ffn_w, norm_w, None, None)
        for g in range(N_ATTN_GROUPS):
            kv_p[g].append(kvs[g])
        h_p.append(h)
        bufs = (caches[0][layer], caches[1][layer], caches[2][layer])
        y_sample, kvs, h = decoder_layer(y_sample, mix_w, ffn_w, norm_w, bufs, state_ssm[layer])
        for g in range(N_ATTN_GROUPS):
            kv_s[g].append(kvs[g])
        h_s.append(h)
    kv_w128_prompt = jnp.stack(kv_p[0])
    kv_w512_prompt = jnp.stack(kv_p[1])
    kv_w2048_prompt = jnp.stack(kv_p[2])
    ssm_state_prompt = jnp.stack(h_p)
    kv_w128_sample = jnp.stack(kv_s[0])
    kv_w512_sample = jnp.stack(kv_s[1])
    kv_w2048_sample = jnp.stack(kv_s[2])
    ssm_state_sample = jnp.stack(h_s)
    return (y_prompt, y_sample, kv_w128_prompt, kv_w512_prompt, kv_w2048_prompt, ssm_state_prompt,
            kv_w128_sample, kv_w512_sample, kv_w2048_sample, ssm_state_sample)
```

```python
import functools
import math

import jax
import jax.numpy as jnp
from jax import lax
from jax.experimental import pallas as pl
from jax.experimental.pallas import tpu as pltpu

ATTN_PATTERNS = ((128, 1), (512, 4), (2048, 16))
N_GROUPS = len(ATTN_PATTERNS)
HEADS = 4
HD = 128
GROUP_W = HEADS * HD
ATTN_W = N_GROUPS * GROUP_W
N_HEADS = N_GROUPS * HEADS
ALIBI_SLOPES = tuple(2.0 ** (-8.0 * (h + 1) / N_HEADS) for h in range(N_HEADS))
QB = 128
SSM_C = 16
SSM_N = 64
TOP_K = 4
SWIGLU_LIMIT = 7.0
SWIGLU_ALPHA = 1.702
LN_EPS = 1e-5

LANES = 128
SUBLANES = 8
VMEM_LIMIT_BYTES = 56 * 1024 * 1024

SSM_CHUNK = 16
SSM_GROUP_BLOCK = 8
SAMPLE_ROWS = 16
MOE_SUB = 256
MOE_CHUNK_SUBS = 8
MOE_FF_TILE = 256
MXU_DTYPE = jnp.bfloat16
NEG_BIG = -1e30


def _params(sem):
    return pltpu.CompilerParams(dimension_semantics=sem, vmem_limit_bytes=VMEM_LIMIT_BYTES)


def _mxu(a):
    return a.astype(MXU_DTYPE)


def _dot(a, b):
    return jnp.dot(a, b, preferred_element_type=jnp.float32)


def _mm_kernel(x_ref, w_ref, o_ref):
    o_ref[...] = _dot(_mxu(x_ref[...]), w_ref[...])


def _matmul(x, w, tm, tn):
    m, k = x.shape
    n = w.shape[1]
    assert m % tm == 0 and n % tn == 0
    return pl.pallas_call(
        _mm_kernel,
        grid=(m // tm, n // tn),
        in_specs=[pl.BlockSpec((tm, k), lambda i, j: (i, 0)),
                  pl.BlockSpec((k, tn), lambda i, j: (0, j))],
        out_specs=pl.BlockSpec((tm, tn), lambda i, j: (i, j)),
        out_shape=jax.ShapeDtypeStruct((m, n), jnp.float32),
        compiler_params=_params(("parallel", "arbitrary")),
        name="in_proj",
    )(x, w)


def _attn_prompt_kernel(q_ref, kc_ref, kp_ref, vc_ref, vp_ref, o_ref, kf, vf, *, tile, slopes, dilation):
    n = pl.program_id(2)
    kf[0:QB, :] = _mxu(kp_ref[...])
    kf[QB:, :] = _mxu(kc_ref[...])
    vf[0:QB, :] = _mxu(vp_ref[...])
    vf[QB:, :] = _mxu(vc_ref[...])
    a = lax.broadcasted_iota(jnp.int32, (QB, 2 * QB), 0)
    c = lax.broadcasted_iota(jnp.int32, (QB, 2 * QB), 1)
    rel = a + QB - c
    band = (rel >= 0) & (rel <= QB)
    first_lo = jnp.where(n > 0, 0, QB)
    dist = (rel * dilation).astype(jnp.float32)
    scale = HD ** -0.5
    for i in range(tile // QB):
        valid = (band & (c >= first_lo)) if i == 0 else band
        for h in range(HEADS):
            q = _mxu(q_ref[i * QB:(i + 1) * QB, h * HD:(h + 1) * HD])
            k = kf[i * QB:(i + 2) * QB, h * HD:(h + 1) * HD]
            v = vf[i * QB:(i + 2) * QB, h * HD:(h + 1) * HD]
            s = lax.dot_general(q, k, (((1,), (1,)), ((), ())), preferred_element_type=jnp.float32) * scale
            s = jnp.where(valid, s - slopes[h] * dist, NEG_BIG)
            m = jnp.max(s, axis=-1, keepdims=True)
            p = jnp.exp(s - m)
            l = jnp.sum(p, axis=-1, keepdims=True)
            o = _dot(_mxu(p), v) / l
            lse = m + jnp.log(l)
            o_ref[i * QB:(i + 1) * QB, 2 * h * HD:(2 * h + 1) * HD] = o
            o_ref[i * QB:(i + 1) * QB, (2 * h + 1) * HD:(2 * h + 2) * HD] = jnp.broadcast_to(lse, (QB, HD))


def _attn_prompt(z, bsz, seq, g):
    window, dilation = ATTN_PATTERNS[g]
    assert window // dilation == QB
    zc = z.shape[1]
    res_len = seq // dilation
    tile = min(4 * QB, res_len)
    assert res_len % tile == 0 and zc % GROUP_W == 0
    zr = z.reshape(bsz, res_len, dilation * zc)
    cpb = zc // GROUP_W
    qcol, kcol, vcol = g, ATTN_W // GROUP_W + g, 2 * ATTN_W // GROUP_W + g
    sub = tile // QB

    def cur(col):
        return pl.BlockSpec((None, tile, GROUP_W), lambda b, r, n: (b, n, r * cpb + col))

    def prev(col):
        return pl.BlockSpec((None, QB, GROUP_W), lambda b, r, n: (b, jnp.maximum(n * sub - 1, 0), r * cpb + col))

    slopes = ALIBI_SLOPES[g * HEADS:(g + 1) * HEADS]
    out = pl.pallas_call(
        functools.partial(_attn_prompt_kernel, tile=tile, slopes=slopes, dilation=dilation),
        grid=(bsz, dilation, res_len // tile),
        in_specs=[cur(qcol), cur(kcol), prev(kcol), cur(vcol), prev(vcol)],
        out_specs=pl.BlockSpec((None, tile, 2 * GROUP_W), lambda b, r, n: (b, n, r)),
        out_shape=jax.ShapeDtypeStruct((bsz, res_len, dilation * 2 * GROUP_W), jnp.float32),
        scratch_shapes=[pltpu.VMEM((tile + QB, GROUP_W), MXU_DTYPE),
                        pltpu.VMEM((tile + QB, GROUP_W), MXU_DTYPE)],
        compiler_params=_params(("parallel", "parallel", "arbitrary")),
        name="attn_prompt_g%d" % g,
    )(zr, zr, zr, zr, zr)
    return out.reshape(bsz * seq, 2 * GROUP_W)


def _attn_sample_kernel(z_ref, c0_ref, c1_ref, c2_ref, o0_ref, o1_ref, o2_ref):
    rows = SAMPLE_ROWS
    scale = HD ** -0.5
    for g, (c_ref, o_ref) in enumerate(((c0_ref, o0_ref), (c1_ref, o1_ref), (c2_ref, o2_ref))):
        window, dilation = ATTN_PATTERNS[g]
        n_past = c_ref.shape[0]
        s_idx = lax.broadcasted_iota(jnp.int32, (rows, n_past), 0)
        i_idx = lax.broadcasted_iota(jnp.int32, (rows, n_past), 1)
        rel_c = n_past + s_idx - i_idx
        valid_c = (rel_c <= window) & ((rel_c & (dilation - 1)) == 0)
        dist_c = rel_c.astype(jnp.float32)
        s2 = lax.broadcasted_iota(jnp.int32, (rows, rows), 0)
        t2 = lax.broadcasted_iota(jnp.int32, (rows, rows), 1)
        rel_n = s2 - t2
        valid_n = (rel_n >= 0) & ((rel_n & (dilation - 1)) == 0)
        dist_n = rel_n.astype(jnp.float32)
        for h in range(HEADS):
            slope = ALIBI_SLOPES[g * HEADS + h]
            col = g * GROUP_W + h * HD
            q = _mxu(z_ref[:, col:col + HD])
            kn = _mxu(z_ref[:, ATTN_W + col:ATTN_W + col + HD])
            vn = _mxu(z_ref[:, 2 * ATTN_W + col:2 * ATTN_W + col + HD])
            kc = _mxu(c_ref[:, h * HD:(h + 1) * HD])
            vc = _mxu(c_ref[:, GROUP_W + h * HD:GROUP_W + (h + 1) * HD])
            sc = lax.dot_general(q, kc, (((1,), (1,)), ((), ())), preferred_element_type=jnp.float32) * scale
            sn = lax.dot_general(q, kn, (((1,), (1,)), ((), ())), preferred_element_type=jnp.float32) * scale
            sc = jnp.where(valid_c, sc - slope * dist_c, NEG_BIG)
            sn = jnp.where(valid_n, sn - slope * dist_n, NEG_BIG)
            m = jnp.maximum(jnp.max(sc, axis=-1, keepdims=True), jnp.max(sn, axis=-1, keepdims=True))
            pc = jnp.exp(sc - m)
            pn = jnp.exp(sn - m)
            l = jnp.sum(pc, axis=-1, keepdims=True) + jnp.sum(pn, axis=-1, keepdims=True)
            o = (_dot(_mxu(pc), vc) + _dot(_mxu(pn), vn)) / l
            lse = m + jnp.log(l)
            o_ref[:, 2 * h * HD:(2 * h + 1) * HD] = o
            o_ref[:, (2 * h + 1) * HD:(2 * h + 2) * HD] = jnp.broadcast_to(lse, (rows, HD))


def _attn_sample(zs, caches, layer, nreq):
    zc = zs.shape[1]
    z3 = zs.reshape(nreq, SAMPLE_ROWS, zc)
    cache_specs, cache_args = [], []
    for cch in caches:
        depth, nb, n_past = cch.shape[:3]
        cache_args.append(cch.reshape(depth, nb, n_past, 2 * GROUP_W))
        cache_specs.append(pl.BlockSpec((None, None, n_past, 2 * GROUP_W), lambda b: (layer, b, 0, 0)))
    out_spec = pl.BlockSpec((None, SAMPLE_ROWS, 2 * GROUP_W), lambda b: (b, 0, 0))
    out_shape = jax.ShapeDtypeStruct((nreq, SAMPLE_ROWS, 2 * GROUP_W), jnp.float32)
    outs = pl.pallas_call(
        _attn_sample_kernel,
        grid=(nreq,),
        in_specs=[pl.BlockSpec((None, SAMPLE_ROWS, zc), lambda b: (b, 0, 0))] + cache_specs,
        out_specs=[out_spec] * N_GROUPS,
        out_shape=[out_shape] * N_GROUPS,
        compiler_params=_params(("parallel",)),
        name="attn_sample",
    )(z3, *cache_args)
    return [o.reshape(nreq * SAMPLE_ROWS, 2 * GROUP_W) for o in outs]


def _ssm_operators(lam_re, lam_im, log_dt, b_re, b_im, c_re, c_im, chunk):
    f32 = jnp.float32
    hi = lax.Precision.HIGHEST
    lam_re, lam_im = lam_re.astype(f32), lam_im.astype(f32)
    dt = jnp.exp(log_dt.astype(f32))[:, None]
    decay = jnp.exp(lam_re * dt)
    a_re = decay * jnp.cos(lam_im * dt)
    a_im = decay * jnp.sin(lam_im * dt)
    den = lam_re * lam_re + lam_im * lam_im
    f_re = ((a_re - 1.0) * lam_re + a_im * lam_im) / den
    f_im = (a_im * lam_re - (a_re - 1.0) * lam_im) / den
    b_re, b_im = b_re.astype(f32), b_im.astype(f32)
    bb_re = f_re[..., None] * b_re - f_im[..., None] * b_im
    bb_im = f_re[..., None] * b_im + f_im[..., None] * b_re
    c_re, c_im = c_re.astype(f32), c_im.astype(f32)

    def power(p):
        p = jnp.asarray(p, f32)[..., None, None]
        mag = jnp.exp(lam_re * dt * p)
        ang = lam_im * dt * p
        return mag * jnp.cos(ang), mag * jnp.sin(ang)

    n_g = lam_re.shape[0]
    ap_re, ap_im = power(jnp.arange(chunk + 1))
    abb_re = ap_re[:chunk, :, :, None] * bb_re - ap_im[:chunk, :, :, None] * bb_im
    abb_im = ap_re[:chunk, :, :, None] * bb_im + ap_im[:chunk, :, :, None] * bb_re
    kern = (jnp.einsum('gcn,tgnd->tgcd', c_re, abb_re, precision=hi)
            - jnp.einsum('gcn,tgnd->tgcd', c_im, abb_im, precision=hi))
    kern = jnp.concatenate([kern, jnp.zeros_like(kern[:1])], axis=0)
    lag = jnp.arange(chunk)[:, None] - jnp.arange(chunk)[None, :]
    m_op = kern[jnp.where(lag >= 0, lag, chunk)]
    m_op = m_op.transpose(2, 0, 3, 1, 4).reshape(n_g, chunk * SSM_C, chunk * SSM_C)
    rev = chunk - 1 - jnp.arange(chunk)
    p_re = abb_re[rev].transpose(1, 2, 0, 3).reshape(n_g, SSM_N, chunk * SSM_C)
    p_im = abb_im[rev].transpose(1, 2, 0, 3).reshape(n_g, SSM_N, chunk * SSM_C)
    p_op = jnp.concatenate([p_re, p_im], axis=1)
    a1_re, a1_im = ap_re[1:], ap_im[1:]
    q_re = c_re[None] * a1_re[:, :, None, :] - c_im[None] * a1_im[:, :, None, :]
    q_im = -(c_re[None] * a1_im[:, :, None, :] + c_im[None] * a1_re[:, :, None, :])
    q_op = jnp.concatenate([q_re, q_im], axis=-1).transpose(1, 0, 2, 3).reshape(n_g, chunk * SSM_C, 2 * SSM_N)
    return m_op, p_op, q_op, power


def _ssm_prompt_kernel(u_ref, m_ref, p_ref, q_ref, pw_ref, y_ref, h_ref, *, chunks_per_seq, n_steps):
    n_cols = u_ref.shape[2]
    lane = lax.broadcasted_iota(jnp.int32, (SSM_N, n_cols), 1) % chunks_per_seq
    for gi in range(u_ref.shape[0]):
        u = u_ref[gi]
        xc = _dot(p_ref[gi], u)
        h_re, h_im = xc[:SSM_N], xc[SSM_N:]
        for j in range(n_steps):
            sh = 1 << j
            a_re = pw_ref[gi, :, j:j + 1]
            a_im = pw_ref[gi, :, n_steps + j:n_steps + j + 1]
            keep = lane >= sh
            s_re = jnp.where(keep, pltpu.roll(h_re, sh, 1), 0.0)
            s_im = jnp.where(keep, pltpu.roll(h_im, sh, 1), 0.0)
            h_re, h_im = h_re + a_re * s_re - a_im * s_im, h_im + a_re * s_im + a_im * s_re
        h_ref[gi, 0:SSM_N, :] = h_re
        h_ref[gi, SSM_N:, :] = h_im
        first = lane >= 1
        hp = jnp.concatenate([jnp.where(first, pltpu.roll(h_re, 1, 1), 0.0),
                              jnp.where(first, pltpu.roll(h_im, 1, 1), 0.0)], axis=0)
        y_ref[gi] = _dot(m_ref[gi], u) + _dot(q_ref[gi], _mxu(hp))


def _ssm_prompt(u, bsz, seq, ops):
    m_op, p_op, q_op, power = ops
    n_g = m_op.shape[0]
    chunks_per_seq = seq // SSM_CHUNK
    n_cols = bsz * chunks_per_seq
    n_steps = int(math.log2(chunks_per_seq))
    assert (1 << n_steps) == chunks_per_seq and n_cols % LANES == 0
    lc = SSM_CHUNK * SSM_C
    ut = _mxu(u).reshape(bsz, chunks_per_seq, SSM_CHUNK, n_g, SSM_C).transpose(3, 2, 4, 0, 1).reshape(n_g, lc, n_cols)
    pw_re, pw_im = power(SSM_CHUNK * (2 ** jnp.arange(n_steps)))
    pw = jnp.concatenate([pw_re, pw_im], axis=0).transpose(1, 2, 0)
    gb = SSM_GROUP_BLOCK
    assert n_g % gb == 0
    y, h = pl.pallas_call(
        functools.partial(_ssm_prompt_kernel, chunks_per_seq=chunks_per_seq, n_steps=n_steps),
        grid=(n_g // gb,),
        in_specs=[pl.BlockSpec((gb, lc, n_cols), lambda i: (i, 0, 0)),
                  pl.BlockSpec((gb, lc, lc), lambda i: (i, 0, 0)),
                  pl.BlockSpec((gb, 2 * SSM_N, lc), lambda i: (i, 0, 0)),
                  pl.BlockSpec((gb, lc, 2 * SSM_N), lambda i: (i, 0, 0)),
                  pl.BlockSpec((gb, SSM_N, 2 * n_steps), lambda i: (i, 0, 0))],
        out_specs=[pl.BlockSpec((gb, lc, n_cols), lambda i: (i, 0, 0)),
                   pl.BlockSpec((gb, 2 * SSM_N, n_cols), lambda i: (i, 0, 0))],
        out_shape=[jax.ShapeDtypeStruct((n_g, lc, n_cols), jnp.float32),
                   jax.ShapeDtypeStruct((n_g, 2 * SSM_N, n_cols), jnp.float32)],
        compiler_params=_params(("parallel",)),
        name="ssm_prompt",
    )(ut, _mxu(m_op), _mxu(p_op), _mxu(q_op), pw)
    y = y.reshape(n_g, SSM_CHUNK, SSM_C, bsz, chunks_per_seq).transpose(3, 4, 1, 0, 2).reshape(bsz * seq, n_g * SSM_C)
    h_last = h.reshape(n_g, 2, SSM_N, bsz, chunks_per_seq)[..., -1].transpose(3, 0, 2, 1)
    return y, h_last


def _ssm_sample_kernel(u_ref, h0_ref, m_ref, p_ref, q_ref, a_ref, y_ref, h_ref):
    for gi in range(u_ref.shape[0]):
        u = u_ref[gi]
        h0 = h0_ref[gi]
        h0_re, h0_im = h0[:SSM_N], h0[SSM_N:]
        a_re = a_ref[gi, :, 0:1]
        a_im = a_ref[gi, :, 1:2]
        xc = _dot(p_ref[gi], u)
        h_ref[gi, 0:SSM_N, :] = a_re * h0_re - a_im * h0_im + xc[:SSM_N]
        h_ref[gi, SSM_N:, :] = a_re * h0_im + a_im * h0_re + xc[SSM_N:]
        y_ref[gi] = _dot(m_ref[gi], u) + _dot(q_ref[gi], _mxu(h0))


def _ssm_sample(u, h0, nreq, n_new, ops):
    m_op, p_op, q_op, power = ops
    n_g = m_op.shape[0]
    lc = n_new * SSM_C
    ut = u.reshape(nreq, SAMPLE_ROWS, n_g, SSM_C)[:, :n_new].transpose(2, 1, 3, 0).reshape(n_g, lc, nreq)
    ut = _mxu(jnp.pad(ut, ((0, 0), (0, 0), (0, LANES - nreq))))
    h0t = h0.astype(jnp.float32).transpose(1, 3, 2, 0).reshape(n_g, 2 * SSM_N, nreq)
    h0t = jnp.pad(h0t, ((0, 0), (0, 0), (0, LANES - nreq)))
    a_re, a_im = power(n_new)
    a_op = jnp.stack([a_re, a_im], axis=-1)
    gb = SSM_GROUP_BLOCK
    y, h = pl.pallas_call(
        _ssm_sample_kernel,
        grid=(n_g // gb,),
        in_specs=[pl.BlockSpec((gb, lc, LANES), lambda i: (i, 0, 0)),
                  pl.BlockSpec((gb, 2 * SSM_N, LANES), lambda i: (i, 0, 0)),
                  pl.BlockSpec((gb, lc, lc), lambda i: (i, 0, 0)),
                  pl.BlockSpec((gb, 2 * SSM_N, lc), lambda i: (i, 0, 0)),
                  pl.BlockSpec((gb, lc, 2 * SSM_N), lambda i: (i, 0, 0)),
                  pl.BlockSpec((gb, SSM_N, 2), lambda i: (i, 0, 0))],
        out_specs=[pl.BlockSpec((gb, lc, LANES), lambda i: (i, 0, 0)),
                   pl.BlockSpec((gb, 2 * SSM_N, LANES), lambda i: (i, 0, 0))],
        out_shape=[jax.ShapeDtypeStruct((n_g, lc, LANES), jnp.float32),
                   jax.ShapeDtypeStruct((n_g, 2 * SSM_N, LANES), jnp.float32)],
        compiler_params=_params(("parallel",)),
        name="ssm_sample",
    )(ut, h0t, _mxu(m_op), _mxu(p_op), _mxu(q_op), a_op)
    y = y[:, :, :nreq].reshape(n_g, n_new, SSM_C, nreq).transpose(3, 1, 0, 2).reshape(nreq, n_new, n_g * SSM_C)
    y = jnp.pad(y, ((0, 0), (0, SAMPLE_ROWS - n_new), (0, 0))).reshape(nreq * SAMPLE_ROWS, n_g * SSM_C)
    h_new = h[:, :, :nreq].reshape(n_g, 2, SSM_N, nreq).transpose(3, 0, 2, 1)
    return y, h_new


def _mix_kernel(p0_ref, p1_ref, p2_ref, ys_ref, u_ref, ga_ref, gs_ref, dsk_ref, wglu_ref, bglu_ref,
                wap_ref, wsp_ref, o_ref, attn_buf):
    parts = (p0_ref, p1_ref, p2_ref)
    for j in range(HEADS):
        lo = 2 * j * HD
        lses = [p[:, lo + HD:lo + 2 * HD] for p in parts]
        top = jnp.maximum(jnp.maximum(lses[0], lses[1]), lses[2])
        ws = [jnp.exp(l - top) for l in lses]
        num = ws[0] * parts[0][:, lo:lo + HD] + ws[1] * parts[1][:, lo:lo + HD] + ws[2] * parts[2][:, lo:lo + HD]
        attn_buf[:, j * HD:(j + 1) * HD] = _mxu(num / (ws[0] + ws[1] + ws[2]))
    y = jax.nn.gelu(ys_ref[...] + dsk_ref[...] * u_ref[...])
    ssm = y * jax.nn.sigmoid(_dot(_mxu(y), wglu_ref[...]) + bglu_ref[...])
    merged = (jax.nn.sigmoid(ga_ref[...]) * _dot(attn_buf[...], wap_ref[...])
              + jax.nn.sigmoid(gs_ref[...]) * _dot(_mxu(ssm), wsp_ref[...]))
    o_ref[...] = _mxu(merged)


def _mix(parts, ys, z, d_skip, w_glu, b_glu, w_ap, w_sp, tm):
    m = ys.shape[0]
    ssm_w = ys.shape[1]
    d_model = w_ap.shape[1]
    u_off, ga_off = 3 * ATTN_W, 3 * ATTN_W + ssm_w
    gs_off = ga_off + d_model
    assert u_off % ssm_w == 0 and ga_off % d_model == 0 and m % tm == 0
    row = lambda w: pl.BlockSpec((tm, w), lambda i: (i, 0))
    const = lambda a: pl.BlockSpec(a.shape, lambda i: (0,) * a.ndim)
    d_skip, b_glu = d_skip.reshape(1, -1), b_glu.reshape(1, -1)
    return pl.pallas_call(
        _mix_kernel,
        grid=(m // tm,),
        in_specs=[row(2 * GROUP_W)] * 3 + [row(ssm_w),
                  pl.BlockSpec((tm, ssm_w), lambda i: (i, u_off // ssm_w)),
                  pl.BlockSpec((tm, d_model), lambda i: (i, ga_off // d_model)),
                  pl.BlockSpec((tm, d_model), lambda i: (i, gs_off // d_model)),
                  const(d_skip), const(w_glu), const(b_glu), const(w_ap), const(w_sp)],
        out_specs=row(d_model),
        out_shape=jax.ShapeDtypeStruct((m, d_model), MXU_DTYPE),
        scratch_shapes=[pltpu.VMEM((tm, GROUP_W), MXU_DTYPE)],
        compiler_params=_params(("parallel",)),
        name="mix",
    )(*parts, ys, z, z, z, d_skip, w_glu, b_glu, w_ap, w_sp)


def _layer_norm(h, g, b):
    mu = jnp.mean(h, axis=-1, keepdims=True)
    d = h - mu
    var = jnp.mean(d * d, axis=-1, keepdims=True)
    return d * lax.rsqrt(var + LN_EPS) * g + b


def _out_kernel(m_ref, x_ref, wo_ref, g_ref, b_ref, rw_ref, rb_ref, x1_ref, x1b_ref, lg_ref, *, alpha):
    h = alpha * x_ref[...] + _dot(m_ref[...], wo_ref[...])
    x1 = _layer_norm(h, g_ref[...], b_ref[...])
    x1_ref[...] = x1
    x1b_ref[...] = _mxu(x1)
    lg_ref[...] = jnp.dot(x1, rw_ref[...], preferred_element_type=jnp.float32,
                          precision=lax.Precision.HIGHEST) + rb_ref[...]


def _out_proj(merged, x, w_o, g, b, rw, rb, alpha, tm):
    m, d_model = x.shape
    assert m % tm == 0
    row = lambda w: pl.BlockSpec((tm, w), lambda i: (i, 0))
    const = lambda a: pl.BlockSpec(a.shape, lambda i: (0,) * a.ndim)
    g, b = g.reshape(1, -1), b.reshape(1, -1)
    return pl.pallas_call(
        functools.partial(_out_kernel, alpha=alpha),
        grid=(m // tm,),
        in_specs=[row(d_model), row(d_model), const(w_o), const(g), const(b), const(rw), const(rb)],
        out_specs=[row(d_model), row(d_model), row(LANES)],
        out_shape=[jax.ShapeDtypeStruct((m, d_model), jnp.float32),
                   jax.ShapeDtypeStruct((m, d_model), MXU_DTYPE),
                   jax.ShapeDtypeStruct((m, LANES), jnp.float32)],
        compiler_params=_params(("parallel",)),
        name="out_proj",
    )(merged, x, w_o, g, b, rw, rb)


def _moe_kernel(ce_ref, cr_ref, cn_ref, tot_ref, x_hbm, wg_ref, wl_ref, wd_ref, bg_ref, bl_ref, bd_ref,
                y_hbm, xbuf, acc, wgb, wlb, wdb, sem_in, sem_out, *, n_ff_tiles, n_sub_total):
    c = pl.program_id(0)
    j = pl.program_id(1)
    nsb = cn_ref[c]
    row0 = cr_ref[c]

    def sub_rows(s):
        return pl.ds(pl.multiple_of(s * MOE_SUB, MOE_SUB), MOE_SUB)

    def x_copy(s):
        return pltpu.make_async_copy(x_hbm.at[sub_rows(row0 + s)], xbuf.at[sub_rows(s)], sem_in)

    def y_copy(s):
        return pltpu.make_async_copy(acc.at[sub_rows(s)], y_hbm.at[sub_rows(row0 + s)], sem_out)

    def for_subs(n, fn):
        def body(s, carry):
            fn(s)
            return carry
        lax.fori_loop(0, n, body, 0)

    @pl.when((j == 0) & (nsb > 0))
    def _():
        for_subs(nsb, lambda s: x_copy(s).start())
        for_subs(nsb, lambda s: x_copy(s).wait())

    @pl.when(nsb > 0)
    def _():
        wgb[...] = _mxu(wg_ref[...])
        wlb[...] = _mxu(wl_ref[...])
        wdb[...] = _mxu(wd_ref[...])

        def sub(s):
            r = pl.multiple_of(s * MOE_SUB, MOE_SUB)
            xs = xbuf[pl.ds(r, MOE_SUB), :]
            glu = jnp.minimum(_dot(xs, wgb[...]) + bg_ref[...], SWIGLU_LIMIT)
            lin = jnp.clip(_dot(xs, wlb[...]) + bl_ref[...], -SWIGLU_LIMIT, SWIGLU_LIMIT)
            act = glu * jax.nn.sigmoid(SWIGLU_ALPHA * glu) * (lin + 1.0)
            part = _dot(_mxu(act), wdb[...])

            @pl.when(j == 0)
            def _():
                acc[pl.ds(r, MOE_SUB), :] = part + bd_ref[...]

            @pl.when(j > 0)
            def _():
                acc[pl.ds(r, MOE_SUB), :] += part

        for_subs(nsb, sub)

    @pl.when((j == n_ff_tiles - 1) & (nsb > 0))
    def _():
        for_subs(nsb, lambda s: y_copy(s).start())
        for_subs(nsb, lambda s: y_copy(s).wait())

    @pl.when((c == pl.num_programs(0) - 1) & (j == n_ff_tiles - 1))
    def _():
        first = tot_ref[0] // MOE_SUB
        acc[0:MOE_SUB, :] = jnp.zeros((MOE_SUB, acc.shape[1]), acc.dtype)

        def z_copy(s):
            return pltpu.make_async_copy(acc.at[pl.ds(0, MOE_SUB)], y_hbm.at[sub_rows(first + s)], sem_out)

        for_subs(n_sub_total - first, lambda s: z_copy(s).start())
        for_subs(n_sub_total - first, lambda s: z_copy(s).wait())


def _moe_experts(x_sorted, chunk_expert, chunk_row, chunk_subs, total_rows, w_gu, b_gu, w_dn, b_dn, layer):
    n_slots, d_model = x_sorted.shape
    d_ff = w_dn.shape[2]
    tf = MOE_FF_TILE
    n_ff_tiles = d_ff // tf
    n_chunks = chunk_expert.shape[0]
    rows = MOE_SUB * MOE_CHUNK_SUBS
    n_exp = w_gu.shape[1]
    b_gu4 = b_gu.reshape(b_gu.shape[0], n_exp, 1, 2 * d_ff)
    b_dn4 = b_dn.reshape(b_dn.shape[0], n_exp, 1, d_model)

    def ff(j, c, cn):
        return jnp.where(cn[c] > 0, j, n_ff_tiles - 1)

    grid_spec = pltpu.PrefetchScalarGridSpec(
        num_scalar_prefetch=4,
        grid=(n_chunks, n_ff_tiles),
        in_specs=[
            pl.BlockSpec(memory_space=pl.ANY),
            pl.BlockSpec((None, None, d_model, tf), lambda c, j, ce, cr, cn, tot: (layer, ce[c], 0, ff(j, c, cn))),
            pl.BlockSpec((None, None, d_model, tf),
                         lambda c, j, ce, cr, cn, tot: (layer, ce[c], 0, n_ff_tiles + ff(j, c, cn))),
            pl.BlockSpec((None, None, tf, d_model), lambda c, j, ce, cr, cn, tot: (layer, ce[c], ff(j, c, cn), 0)),
            pl.BlockSpec((None, None, 1, tf), lambda c, j, ce, cr, cn, tot: (layer, ce[c], 0, ff(j, c, cn))),
            pl.BlockSpec((None, None, 1, tf),
                         lambda c, j, ce, cr, cn, tot: (layer, ce[c], 0, n_ff_tiles + ff(j, c, cn))),
            pl.BlockSpec((None, None, 1, d_model), lambda c, j, ce, cr, cn, tot: (layer, ce[c], 0, 0)),
        ],
        out_specs=pl.BlockSpec(memory_space=pl.ANY),
        scratch_shapes=[
            pltpu.VMEM((rows, d_model), MXU_DTYPE),
            pltpu.VMEM((rows, d_model), jnp.float32),
            pltpu.VMEM((d_model, tf), MXU_DTYPE),
            pltpu.VMEM((d_model, tf), MXU_DTYPE),
            pltpu.VMEM((tf, d_model), MXU_DTYPE),
            pltpu.SemaphoreType.DMA(()),
            pltpu.SemaphoreType.DMA(()),
        ],
    )
    return pl.pallas_call(
        functools.partial(_moe_kernel, n_ff_tiles=n_ff_tiles, n_sub_total=n_slots // MOE_SUB),
        grid_spec=grid_spec,
        out_shape=jax.ShapeDtypeStruct((n_slots, d_model), jnp.float32),
        compiler_params=_params(("arbitrary", "arbitrary")),
        name="moe_experts",
    )(chunk_expert, chunk_row, chunk_subs, total_rows, x_sorted, w_gu, w_gu, w_dn, b_gu4, b_gu4, b_dn4)


def _route(logits, n_exp):
    n_tok = logits.shape[0]
    top_val, top_idx = lax.top_k(logits[:, :n_exp], TOP_K)
    gate = jax.nn.softmax(top_val, axis=-1)
    n_assign = n_tok * TOP_K
    n_slots = -(-(n_assign + n_exp * (MOE_SUB - 1)) // MOE_SUB) * MOE_SUB
    flat_e = top_idx.reshape(-1)
    onehot = (flat_e[:, None] == jnp.arange(n_exp, dtype=flat_e.dtype)[None, :]).astype(jnp.int32)
    before = jnp.cumsum(onehot, axis=0) - onehot
    rank = jnp.take_along_axis(before, flat_e[:, None], axis=1)[:, 0]
    counts = onehot.sum(axis=0)
    padded = (counts + MOE_SUB - 1) // MOE_SUB * MOE_SUB
    pad_end = jnp.cumsum(padded)
    pad_start = pad_end - padded
    dest = (pad_start[flat_e] + rank).astype(jnp.int32)
    slot_tok = jnp.zeros((n_slots,), jnp.int32).at[dest].set(jnp.arange(n_assign, dtype=jnp.int32) // TOP_K)
    rows = MOE_SUB * MOE_CHUNK_SUBS
    n_chunks = n_slots // rows + n_exp + 1
    per_exp = (padded + rows - 1) // rows
    chunk_end = jnp.cumsum(per_exp)
    cid = jnp.arange(n_chunks)
    exp_of = jnp.minimum(jnp.searchsorted(chunk_end, cid, side='right'), n_exp - 1)
    within = cid - (chunk_end - per_exp)[exp_of]
    live = cid < chunk_end[-1]
    row0 = pad_start[exp_of] + within * rows
    subs = jnp.clip(padded[exp_of] - within * rows, 0, rows) // MOE_SUB
    last_exp = exp_of[jnp.maximum(chunk_end[-1] - 1, 0)]
    chunk_expert = jnp.where(live, exp_of, last_exp).astype(jnp.int32)
    chunk_row = (jnp.where(live, row0, 0) // MOE_SUB).astype(jnp.int32)
    chunk_subs = jnp.where(live, subs, 0).astype(jnp.int32)
    total_rows = pad_end[-1:].astype(jnp.int32)
    return gate, dest, slot_tok, chunk_expert, chunk_row, chunk_subs, total_rows


def _combine_kernel(x_ref, yb_ref, gate_ref, g_ref, b_ref, o_ref, *, alpha):
    d_model = x_ref.shape[1]
    h = alpha * x_ref[...]
    for k in range(TOP_K):
        h = h + gate_ref[:, k:k + 1] * yb_ref[:, k * d_model:(k + 1) * d_model]
    o_ref[...] = _layer_norm(h, g_ref[...], b_ref[...])


def _combine(x1, yb_tok, gate, g, b, alpha, tm):
    m, d_model = x1.shape
    assert m % tm == 0
    row = lambda w: pl.BlockSpec((tm, w), lambda i: (i, 0))
    const = lambda a: pl.BlockSpec(a.shape, lambda i: (0,) * a.ndim)
    g, b = g.reshape(1, -1), b.reshape(1, -1)
    gate = jnp.pad(gate, ((0, 0), (0, LANES - TOP_K)))
    return pl.pallas_call(
        functools.partial(_combine_kernel, alpha=alpha),
        grid=(m // tm,),
        in_specs=[row(d_model), row(TOP_K * d_model), row(LANES), const(g), const(b)],
        out_specs=row(d_model),
        out_shape=jax.ShapeDtypeStruct((m, d_model), jnp.float32),
        compiler_params=_params(("parallel",)),
        name="combine",
    )(x1, yb_tok, gate, g, b)


def _forward(x_prompt, x_sample, cache_kv_w128, cache_kv_w512, cache_kv_w2048, state_ssm, w_in, w_attn_proj,
             ssm_lambda_re, ssm_lambda_im, ssm_log_dt, ssm_b_re, ssm_b_im, ssm_c_re, ssm_c_im, ssm_d, w_glu,
             b_glu, w_ssm_proj, w_o, ln1_g, ln1_b, router_w, router_b, w_gate_up, b_gate_up, w_down, b_down,
             ln2_g, ln2_b):
    caches = (cache_kv_w128, cache_kv_w512, cache_kv_w2048)
    bsz, seq, d_model = x_prompt.shape
    nreq, n_new, _ = x_sample.shape
    depth = w_in.shape[0]
    n_exp = router_w.shape[2]
    alpha = (2 * depth) ** 0.25
    n_p = bsz * seq
    n_s = nreq * SAMPLE_ROWS
    assert n_new <= SAMPLE_ROWS

    xp = x_prompt.reshape(n_p, d_model)
    xs = jnp.pad(x_sample, ((0, 0), (0, SAMPLE_ROWS - n_new), (0, 0))).reshape(n_s, d_model)
    kv_p = [[] for _ in ATTN_PATTERNS]
    kv_s = [[] for _ in ATTN_PATTERNS]
    h_p, h_s = [], []
    tm_in = 1024 if n_p % 1024 == 0 else 256

    for layer in range(depth):
        w_in_l = _mxu(w_in[layer])
        w_glu_l, w_ap_l, w_sp_l, w_o_l = (_mxu(w[layer]) for w in (w_glu, w_attn_proj, w_ssm_proj, w_o))
        rw_l = jnp.pad(router_w[layer], ((0, 0), (0, LANES - n_exp)))
        rb_l = jnp.pad(router_b[layer], (0, LANES - n_exp)).reshape(1, LANES)
        lam = (ssm_lambda_re[layer], ssm_lambda_im[layer], ssm_log_dt[layer], ssm_b_re[layer], ssm_b_im[layer],
               ssm_c_re[layer], ssm_c_im[layer])
        ssm_w = ssm_d.shape[1]
        u_lo = 3 * ATTN_W

        zp = _matmul(xp, w_in_l, tm_in, 1024)
        parts_p = [_attn_prompt(zp, bsz, seq, g) for g in range(N_GROUPS)]
        ys_p, h_last = _ssm_prompt(zp[:, u_lo:u_lo + ssm_w], bsz, seq, _ssm_operators(*lam, SSM_CHUNK))
        h_p.append(h_last)
        z3 = zp.reshape(bsz, seq, -1)
        for g, (window, _) in enumerate(ATTN_PATTERNS):
            keep = min(window, seq)
            k_g = z3[:, seq - keep:, ATTN_W + g * GROUP_W:ATTN_W + (g + 1) * GROUP_W]
            v_g = z3[:, seq - keep:, 2 * ATTN_W + g * GROUP_W:2 * ATTN_W + (g + 1) * GROUP_W]
            kv_p[g].append(jnp.stack([k_g, v_g], axis=2).reshape(bsz, keep, 2, HEADS, HD))
        mg_p = _mix(parts_p, ys_p, zp, ssm_d[layer], w_glu_l, b_glu[layer], w_ap_l, w_sp_l, 128)
        x1_p, x1b_p, lg_p = _out_proj(mg_p, xp, w_o_l, ln1_g[layer], ln1_b[layer], rw_l, rb_l, alpha, 256)

        zs = _matmul(xs, w_in_l, n_s, 1024)
        parts_s = _attn_sample(zs, caches, layer, nreq)
        ys_s, h_new = _ssm_sample(zs[:, u_lo:u_lo + ssm_w], state_ssm[layer], nreq, n_new,
                                  _ssm_operators(*lam, n_new))
        h_s.append(h_new)
        zs3 = zs.reshape(nreq, SAMPLE_ROWS, -1)[:, :n_new]
        for g in range(N_GROUPS):
            k_g = zs3[:, :, ATTN_W + g * GROUP_W:ATTN_W + (g + 1) * GROUP_W]
            v_g = zs3[:, :, 2 * ATTN_W + g * GROUP_W:2 * ATTN_W + (g + 1) * GROUP_W]
            new = jnp.stack([k_g, v_g], axis=2).reshape(nreq, n_new, 2, HEADS, HD)
            old = caches[g][layer]
            kv_s[g].append(jnp.concatenate([old.astype(new.dtype), new], axis=1)[:, -old.shape[1]:])
        mg_s = _mix(parts_s, ys_s, zs, ssm_d[layer], w_glu_l, b_glu[layer], w_ap_l, w_sp_l, n_s)
        x1_s, x1b_s, lg_s = _out_proj(mg_s, xs, w_o_l, ln1_g[layer], ln1_b[layer], rw_l, rb_l, alpha, n_s)

        x1b = jnp.concatenate([x1b_p, x1b_s], axis=0)
        logits = jnp.concatenate([lg_p, lg_s], axis=0)
        gate, dest, slot_tok, c_exp, c_row, c_sub, total = _route(logits, n_exp)
        yb = _moe_experts(x1b[slot_tok], c_exp, c_row, c_sub, total, w_gate_up, b_gate_up, w_down, b_down, layer)
        yb_tok = yb[dest].reshape(n_p + n_s, TOP_K * d_model)
        xp = _combine(x1_p, yb_tok[:n_p], gate[:n_p], ln2_g[layer], ln2_b[layer], alpha, 256)
        xs = _combine(x1_s, yb_tok[n_p:], gate[n_p:], ln2_g[layer], ln2_b[layer], alpha, n_s)

    y_prompt = xp.reshape(bsz, seq, d_model)
    y_sample = xs.reshape(nreq, SAMPLE_ROWS, d_model)[:, :n_new]
    return (y_prompt, y_sample,
            jnp.stack(kv_p[0]), jnp.stack(kv_p[1]), jnp.stack(kv_p[2]), jnp.stack(h_p),
            jnp.stack(kv_s[0]), jnp.stack(kv_s[1]), jnp.stack(kv_s[2]), jnp.stack(h_s))


_forward_jit = jax.jit(_forward)


def kernel(x_prompt, x_sample, cache_kv_w128, cache_kv_w512, cache_kv_w2048, state_ssm, w_in, w_attn_proj,
           ssm_lambda_re, ssm_lambda_im, ssm_log_dt, ssm_b_re, ssm_b_im, ssm_c_re, ssm_c_im, ssm_d, w_glu, b_glu,
           w_ssm_proj, w_o, ln1_g, ln1_b, router_w, router_b, w_gate_up, b_gate_up, w_down, b_down, ln2_g, ln2_b):
    return _forward_jit(x_prompt, x_sample, cache_kv_w128, cache_kv_w512, cache_kv_w2048, state_ssm, w_in,
                        w_attn_proj, ssm_lambda_re, ssm_lambda_im, ssm_log_dt, ssm_b_re, ssm_b_im, ssm_c_re,
                        ssm_c_im, ssm_d, w_glu, b_glu, w_ssm_proj, w_o, ln1_g, ln1_b, router_w, router_b,
                        w_gate_up, b_gate_up, w_down, b_down, ln2_g, ln2_b)
```

```python
import functools
import math

import jax
import jax.numpy as jnp
from jax import lax
from jax.experimental import pallas as pl
from jax.experimental.pallas import tpu as pltpu

ATTN_PATTERNS = ((128, 1), (512, 4), (2048, 16))
N_GROUPS = len(ATTN_PATTERNS)
HEADS = 4
HD = 128
GROUP_W = HEADS * HD
ATTN_W = N_GROUPS * GROUP_W
N_HEADS = N_GROUPS * HEADS
ALIBI_SLOPES = tuple(2.0 ** (-8.0 * (h + 1) / N_HEADS) for h in range(N_HEADS))
QB = 128
ATTN_BLOCK = QB * max(d for _, d in ATTN_PATTERNS)
SSM_C = 16
SSM_N = 64
TOP_K = 4
SWIGLU_LIMIT = 7.0
SWIGLU_ALPHA = 1.702
LN_EPS = 1e-5

LANES = 128
SUBLANES = 8
VMEM_LIMIT_BYTES = 56 * 1024 * 1024

SSM_CHUNK = 16
SSM_GROUP_BLOCK = 8
SAMPLE_ROWS = 16
MOE_SUB = 256
MOE_CHUNK_SUBS = 8
MOE_FF_TILE = 256
MXU_DTYPE = jnp.bfloat16
NEG_BIG = -1e30


def _params(sem):
    return pltpu.CompilerParams(dimension_semantics=sem, vmem_limit_bytes=VMEM_LIMIT_BYTES)


def _mxu(a):
    return a.astype(MXU_DTYPE)


def _dot(a, b):
    return jnp.dot(a, b, preferred_element_type=jnp.float32)


def _mm_kernel(x_ref, w_ref, o_ref):
    o_ref[...] = _dot(_mxu(x_ref[...]), w_ref[...])


def _matmul(x, w, tm, tn):
    m, k = x.shape
    n = w.shape[1]
    assert m % tm == 0 and n % tn == 0
    return pl.pallas_call(
        _mm_kernel,
        grid=(m // tm, n // tn),
        in_specs=[pl.BlockSpec((tm, k), lambda i, j: (i, 0)),
                  pl.BlockSpec((k, tn), lambda i, j: (0, j))],
        out_specs=pl.BlockSpec((tm, tn), lambda i, j: (i, j)),
        out_shape=jax.ShapeDtypeStruct((m, n), jnp.float32),
        compiler_params=_params(("parallel", "arbitrary")),
        name="in_proj",
    )(x, w)


def _attn_prompt_kernel(q_ref, kc_ref, kp_ref, vc_ref, vp_ref, o_ref, l_ref, kf, vf, *, tile, slopes, dilation):
    h = pl.program_id(1)
    n = pl.program_id(2)
    slope = jnp.float32(slopes[HEADS - 1])
    for hh in range(HEADS - 2, -1, -1):
        slope = jnp.where(h == hh, jnp.float32(slopes[hh]), slope)
    a = lax.broadcasted_iota(jnp.int32, (QB, 2 * QB), 0)
    c = lax.broadcasted_iota(jnp.int32, (QB, 2 * QB), 1)
    rel = a + QB - c
    band = (rel >= 0) & (rel <= QB)
    first_lo = jnp.where(n > 0, 0, QB)
    bias = slope * (rel * dilation).astype(jnp.float32)
    scale = HD ** -0.5

    def rows(r, start, count):
        if dilation == 1:
            return pl.ds(start, count)
        return pl.ds(start * dilation + r, count, stride=dilation)

    for r in range(dilation):
        kf[0:QB, :] = _mxu(kp_ref[rows(r, 0, QB), :])
        kf[QB:, :] = _mxu(kc_ref[rows(r, 0, tile), :])
        vf[0:QB, :] = _mxu(vp_ref[rows(r, 0, QB), :])
        vf[QB:, :] = _mxu(vc_ref[rows(r, 0, tile), :])
        for i in range(tile // QB):
            valid = (band & (c >= first_lo)) if i == 0 else band
            q = _mxu(q_ref[rows(r, i * QB, QB), :])
            k = kf[i * QB:(i + 2) * QB, :]
            v = vf[i * QB:(i + 2) * QB, :]
            s = lax.dot_general(q, k, (((1,), (1,)), ((), ())), preferred_element_type=jnp.float32) * scale
            s = jnp.where(valid, s - bias, NEG_BIG)
            m = jnp.max(s, axis=-1, keepdims=True)
            p = jnp.exp(s - m)
            l = jnp.sum(p, axis=-1, keepdims=True)
            o_ref[rows(r, i * QB, QB), :] = _dot(_mxu(p), v) / l
            l_ref[rows(r, i * QB, QB), :] = jnp.broadcast_to(m + jnp.log(l), (QB, HD))


def _attn_prompt(z3, g):
    window, dilation = ATTN_PATTERNS[g]
    assert window // dilation == QB
    bsz, seq, zc = z3.shape
    blk = ATTN_BLOCK
    tile = blk // dilation
    assert seq % blk == 0 and zc % HD == 0
    qcol, kcol, vcol = g * HEADS, (ATTN_W // HD) + g * HEADS, 2 * (ATTN_W // HD) + g * HEADS
    sub = blk // (QB * dilation)

    def cur(col):
        return pl.BlockSpec((None, blk, HD), lambda b, h, n: (b, n, col + h))

    def prev(col):
        return pl.BlockSpec((None, QB * dilation, HD), lambda b, h, n: (b, jnp.maximum(n * sub - 1, 0), col + h))

    slopes = ALIBI_SLOPES[g * HEADS:(g + 1) * HEADS]
    out_spec = pl.BlockSpec((None, blk, HD), lambda b, h, n: (b, n, h))
    out_shape = jax.ShapeDtypeStruct((bsz, seq, GROUP_W), jnp.float32)
    o, lse = pl.pallas_call(
        functools.partial(_attn_prompt_kernel, tile=tile, slopes=slopes, dilation=dilation),
        grid=(bsz, HEADS, seq // blk),
        in_specs=[cur(qcol), cur(kcol), prev(kcol), cur(vcol), prev(vcol)],
        out_specs=[out_spec, out_spec],
        out_shape=[out_shape, out_shape],
        scratch_shapes=[pltpu.VMEM((tile + QB, HD), MXU_DTYPE),
                        pltpu.VMEM((tile + QB, HD), MXU_DTYPE)],
        compiler_params=_params(("parallel", "parallel", "arbitrary")),
        name="attn_prompt_g%d" % g,
    )(z3, z3, z3, z3, z3)
    return o.reshape(bsz * seq, GROUP_W), lse.reshape(bsz * seq, GROUP_W)


def _attn_sample_kernel(z_ref, c0_ref, c1_ref, c2_ref, o0_ref, l0_ref, o1_ref, l1_ref, o2_ref, l2_ref):
    rows = SAMPLE_ROWS
    scale = HD ** -0.5
    for g, (c_ref, o_ref, l_ref) in enumerate(((c0_ref, o0_ref, l0_ref), (c1_ref, o1_ref, l1_ref),
                                                (c2_ref, o2_ref, l2_ref))):
        window, dilation = ATTN_PATTERNS[g]
        n_past = c_ref.shape[0]
        s_idx = lax.broadcasted_iota(jnp.int32, (rows, n_past), 0)
        i_idx = lax.broadcasted_iota(jnp.int32, (rows, n_past), 1)
        rel_c = n_past + s_idx - i_idx
        valid_c = (rel_c <= window) & ((rel_c & (dilation - 1)) == 0)
        dist_c = rel_c.astype(jnp.float32)
        s2 = lax.broadcasted_iota(jnp.int32, (rows, rows), 0)
        t2 = lax.broadcasted_iota(jnp.int32, (rows, rows), 1)
        rel_n = s2 - t2
        valid_n = (rel_n >= 0) & ((rel_n & (dilation - 1)) == 0)
        dist_n = rel_n.astype(jnp.float32)
        for h in range(HEADS):
            slope = ALIBI_SLOPES[g * HEADS + h]
            col = g * GROUP_W + h * HD
            q = _mxu(z_ref[:, col:col + HD])
            kn = _mxu(z_ref[:, ATTN_W + col:ATTN_W + col + HD])
            vn = _mxu(z_ref[:, 2 * ATTN_W + col:2 * ATTN_W + col + HD])
            kc = _mxu(c_ref[:, h * HD:(h + 1) * HD])
            vc = _mxu(c_ref[:, GROUP_W + h * HD:GROUP_W + (h + 1) * HD])
            sc = lax.dot_general(q, kc, (((1,), (1,)), ((), ())), preferred_element_type=jnp.float32) * scale
            sn = lax.dot_general(q, kn, (((1,), (1,)), ((), ())), preferred_element_type=jnp.float32) * scale
            sc = jnp.where(valid_c, sc - slope * dist_c, NEG_BIG)
            sn = jnp.where(valid_n, sn - slope * dist_n, NEG_BIG)
            m = jnp.maximum(jnp.max(sc, axis=-1, keepdims=True), jnp.max(sn, axis=-1, keepdims=True))
            pc = jnp.exp(sc - m)
            pn = jnp.exp(sn - m)
            l = jnp.sum(pc, axis=-1, keepdims=True) + jnp.sum(pn, axis=-1, keepdims=True)
            o = (_dot(_mxu(pc), vc) + _dot(_mxu(pn), vn)) / l
            o_ref[:, h * HD:(h + 1) * HD] = o
            l_ref[:, h * HD:(h + 1) * HD] = jnp.broadcast_to(m + jnp.log(l), (rows, HD))


def _attn_sample(zs, caches, layer, nreq):
    zc = zs.shape[1]
    z3 = zs.reshape(nreq, SAMPLE_ROWS, zc)
    cache_specs, cache_args = [], []
    for cch in caches:
        depth, nb, n_past = cch.shape[:3]
        cache_args.append(cch.reshape(depth, nb, n_past, 2 * GROUP_W))
        cache_specs.append(pl.BlockSpec((None, None, n_past, 2 * GROUP_W), lambda b: (layer, b, 0, 0)))
    out_spec = pl.BlockSpec((None, SAMPLE_ROWS, GROUP_W), lambda b: (b, 0, 0))
    out_shape = jax.ShapeDtypeStruct((nreq, SAMPLE_ROWS, GROUP_W), jnp.float32)
    outs = pl.pallas_call(
        _attn_sample_kernel,
        grid=(nreq,),
        in_specs=[pl.BlockSpec((None, SAMPLE_ROWS, zc), lambda b: (b, 0, 0))] + cache_specs,
        out_specs=[out_spec] * (2 * N_GROUPS),
        out_shape=[out_shape] * (2 * N_GROUPS),
        compiler_params=_params(("parallel",)),
        name="attn_sample",
    )(z3, *cache_args)
    return [o.reshape(nreq * SAMPLE_ROWS, GROUP_W) for o in outs]


def _ssm_operators(lam_re, lam_im, log_dt, b_re, b_im, c_re, c_im, chunk):
    f32 = jnp.float32
    hi = lax.Precision.HIGHEST
    lam_re, lam_im = lam_re.astype(f32), lam_im.astype(f32)
    dt = jnp.exp(log_dt.astype(f32))[:, None]
    decay = jnp.exp(lam_re * dt)
    a_re = decay * jnp.cos(lam_im * dt)
    a_im = decay * jnp.sin(lam_im * dt)
    den = lam_re * lam_re + lam_im * lam_im
    f_re = ((a_re - 1.0) * lam_re + a_im * lam_im) / den
    f_im = (a_im * lam_re - (a_re - 1.0) * lam_im) / den
    b_re, b_im = b_re.astype(f32), b_im.astype(f32)
    bb_re = f_re[..., None] * b_re - f_im[..., None] * b_im
    bb_im = f_re[..., None] * b_im + f_im[..., None] * b_re
    c_re, c_im = c_re.astype(f32), c_im.astype(f32)

    def power(p):
        p = jnp.asarray(p, f32)[..., None, None]
        mag = jnp.exp(lam_re * dt * p)
        ang = lam_im * dt * p
        return mag * jnp.cos(ang), mag * jnp.sin(ang)

    n_g = lam_re.shape[0]
    ap_re, ap_im = power(jnp.arange(chunk + 1))
    abb_re = ap_re[:chunk, :, :, None] * bb_re - ap_im[:chunk, :, :, None] * bb_im
    abb_im = ap_re[:chunk, :, :, None] * bb_im + ap_im[:chunk, :, :, None] * bb_re
    kern = (jnp.einsum('gcn,tgnd->tgcd', c_re, abb_re, precision=hi)
            - jnp.einsum('gcn,tgnd->tgcd', c_im, abb_im, precision=hi))
    kern = jnp.concatenate([kern, jnp.zeros_like(kern[:1])], axis=0)
    lag = jnp.arange(chunk)[:, None] - jnp.arange(chunk)[None, :]
    m_op = kern[jnp.where(lag >= 0, lag, chunk)]
    m_op = m_op.transpose(2, 0, 3, 1, 4).reshape(n_g, chunk * SSM_C, chunk * SSM_C)
    rev = chunk - 1 - jnp.arange(chunk)
    p_re = abb_re[rev].transpose(1, 2, 0, 3).reshape(n_g, SSM_N, chunk * SSM_C)
    p_im = abb_im[rev].transpose(1, 2, 0, 3).reshape(n_g, SSM_N, chunk * SSM_C)
    p_op = jnp.concatenate([p_re, p_im], axis=1)
    a1_re, a1_im = ap_re[1:], ap_im[1:]
    q_re = c_re[None] * a1_re[:, :, None, :] - c_im[None] * a1_im[:, :, None, :]
    q_im = -(c_re[None] * a1_im[:, :, None, :] + c_im[None] * a1_re[:, :, None, :])
    q_op = jnp.concatenate([q_re, q_im], axis=-1).transpose(1, 0, 2, 3).reshape(n_g, chunk * SSM_C, 2 * SSM_N)
    return m_op, p_op, q_op, power


def _u_proj_kernel(x_ref, w_ref, u_ref, slab):
    n_slab, n_tok, _ = slab.shape
    n_chunk = n_tok // SSM_CHUNK
    groups_per_slab = LANES // SSM_C
    for j2 in range(n_slab // 2):
        zt = _dot(_mxu(x_ref[...]), w_ref[:, 2 * j2 * LANES:2 * (j2 + 1) * LANES])
        slab[2 * j2] = zt[:, :LANES]
        slab[2 * j2 + 1] = zt[:, LANES:]
    for j in range(n_slab):
        for s in range(SSM_CHUNK):
            piece = slab[j, pl.ds(s, n_chunk, stride=SSM_CHUNK), :].T
            for gg in range(groups_per_slab):
                u_ref[j * groups_per_slab + gg, s * SSM_C:(s + 1) * SSM_C, :] = _mxu(piece[gg * SSM_C:(gg + 1) * SSM_C, :])


def _u_proj(x, w_u, n_g):
    n_tok, k = x.shape
    blk_tok = LANES * SSM_CHUNK
    halves = 2
    cols = w_u.shape[1] // halves
    assert n_tok % blk_tok == 0 and cols % (2 * LANES) == 0 and w_u.shape[1] == n_g * SSM_C
    lc = SSM_CHUNK * SSM_C
    return pl.pallas_call(
        _u_proj_kernel,
        grid=(n_tok // blk_tok, halves),
        in_specs=[pl.BlockSpec((blk_tok, k), lambda i, hf: (i, 0)),
                  pl.BlockSpec((k, cols), lambda i, hf: (0, hf))],
        out_specs=pl.BlockSpec((n_g // halves, lc, LANES), lambda i, hf: (hf, 0, i)),
        out_shape=jax.ShapeDtypeStruct((n_g, lc, n_tok // SSM_CHUNK), MXU_DTYPE),
        scratch_shapes=[pltpu.VMEM((cols // LANES, blk_tok, LANES), jnp.float32)],
        compiler_params=_params(("parallel", "arbitrary")),
        name="u_proj",
    )(x, w_u)


def _ssm_prompt_kernel(u_ref, m_ref, p_ref, q_ref, pw_ref, y_ref, h_ref, ybuf, *, chunks_per_seq, n_steps):
    n_cols = u_ref.shape[2]
    lane = lax.broadcasted_iota(jnp.int32, (SSM_N, n_cols), 1) % chunks_per_seq
    for gi in range(u_ref.shape[0]):
        u = u_ref[gi]
        xc = _dot(p_ref[gi], u)
        h_re, h_im = xc[:SSM_N], xc[SSM_N:]
        for j in range(n_steps):
            sh = 1 << j
            a_re = pw_ref[gi, :, j:j + 1]
            a_im = pw_ref[gi, :, n_steps + j:n_steps + j + 1]
            keep = lane >= sh
            s_re = jnp.where(keep, pltpu.roll(h_re, sh, 1), 0.0)
            s_im = jnp.where(keep, pltpu.roll(h_im, sh, 1), 0.0)
            h_re, h_im = h_re + a_re * s_re - a_im * s_im, h_im + a_re * s_im + a_im * s_re
        h_ref[gi, 0:SSM_N, :] = h_re
        h_ref[gi, SSM_N:, :] = h_im
        first = lane >= 1
        hp = jnp.concatenate([jnp.where(first, pltpu.roll(h_re, 1, 1), 0.0),
                              jnp.where(first, pltpu.roll(h_im, 1, 1), 0.0)], axis=0)
        ybuf[gi] = _dot(m_ref[gi], u) + _dot(q_ref[gi], _mxu(hp))
    for t in range(SSM_CHUNK):
        rows = jnp.concatenate([ybuf[gi, t * SSM_C:(t + 1) * SSM_C, :] for gi in range(u_ref.shape[0])], axis=0)
        for cb in range(n_cols // LANES):
            y_ref[pl.ds(cb * LANES * SSM_CHUNK + t, LANES, stride=SSM_CHUNK), :] = rows[:, cb * LANES:(cb + 1) * LANES].T


def _ssm_prompt(ut, bsz, seq, ops):
    m_op, p_op, q_op, power = ops
    n_g = m_op.shape[0]
    chunks_per_seq = seq // SSM_CHUNK
    n_cols = bsz * chunks_per_seq
    n_steps = int(math.log2(chunks_per_seq))
    assert (1 << n_steps) == chunks_per_seq and n_cols % LANES == 0
    lc = SSM_CHUNK * SSM_C
    pw_re, pw_im = power(SSM_CHUNK * (2 ** jnp.arange(n_steps)))
    pw = jnp.concatenate([pw_re, pw_im], axis=0).transpose(1, 2, 0)
    gb = LANES // SSM_C
    assert n_g % gb == 0
    y, h = pl.pallas_call(
        functools.partial(_ssm_prompt_kernel, chunks_per_seq=chunks_per_seq, n_steps=n_steps),
        grid=(n_g // gb,),
        in_specs=[pl.BlockSpec((gb, lc, n_cols), lambda i: (i, 0, 0)),
                  pl.BlockSpec((gb, lc, lc), lambda i: (i, 0, 0)),
                  pl.BlockSpec((gb, 2 * SSM_N, lc), lambda i: (i, 0, 0)),
                  pl.BlockSpec((gb, lc, 2 * SSM_N), lambda i: (i, 0, 0)),
                  pl.BlockSpec((gb, SSM_N, 2 * n_steps), lambda i: (i, 0, 0))],
        out_specs=[pl.BlockSpec((bsz * seq, LANES), lambda i: (0, i)),
                   pl.BlockSpec((gb, 2 * SSM_N, n_cols), lambda i: (i, 0, 0))],
        out_shape=[jax.ShapeDtypeStruct((bsz * seq, n_g * SSM_C), jnp.float32),
                   jax.ShapeDtypeStruct((n_g, 2 * SSM_N, n_cols), jnp.float32)],
        scratch_shapes=[pltpu.VMEM((gb, lc, n_cols), jnp.float32)],
        compiler_params=_params(("parallel",)),
        name="ssm_prompt",
    )(ut, _mxu(m_op), _mxu(p_op), _mxu(q_op), pw)
    h_last = h.reshape(n_g, 2, SSM_N, bsz, chunks_per_seq)[..., -1].transpose(3, 0, 2, 1)
    return y, h_last


def _ssm_sample_kernel(u_ref, h0_ref, m_ref, p_ref, q_ref, a_ref, y_ref, h_ref):
    for gi in range(u_ref.shape[0]):
        u = u_ref[gi]
        h0 = h0_ref[gi]
        h0_re, h0_im = h0[:SSM_N], h0[SSM_N:]
        a_re = a_ref[gi, :, 0:1]
        a_im = a_ref[gi, :, 1:2]
        xc = _dot(p_ref[gi], u)
        h_ref[gi, 0:SSM_N, :] = a_re * h0_re - a_im * h0_im + xc[:SSM_N]
        h_ref[gi, SSM_N:, :] = a_re * h0_im + a_im * h0_re + xc[SSM_N:]
        y_ref[gi] = _dot(m_ref[gi], u) + _dot(q_ref[gi], _mxu(h0))


def _ssm_sample(u, h0, nreq, n_new, ops):
    m_op, p_op, q_op, power = ops
    n_g = m_op.shape[0]
    lc = n_new * SSM_C
    ut = u.reshape(nreq, SAMPLE_ROWS, n_g, SSM_C)[:, :n_new].transpose(2, 1, 3, 0).reshape(n_g, lc, nreq)
    ut = _mxu(jnp.pad(ut, ((0, 0), (0, 0), (0, LANES - nreq))))
    h0t = h0.astype(jnp.float32).transpose(1, 3, 2, 0).reshape(n_g, 2 * SSM_N, nreq)
    h0t = jnp.pad(h0t, ((0, 0), (0, 0), (0, LANES - nreq)))
    a_re, a_im = power(n_new)
    a_op = jnp.stack([a_re, a_im], axis=-1)
    gb = SSM_GROUP_BLOCK
    y, h = pl.pallas_call(
        _ssm_sample_kernel,
        grid=(n_g // gb,),
        in_specs=[pl.BlockSpec((gb, lc, LANES), lambda i: (i, 0, 0)),
                  pl.BlockSpec((gb, 2 * SSM_N, LANES), lambda i: (i, 0, 0)),
                  pl.BlockSpec((gb, lc, lc), lambda i: (i, 0, 0)),
                  pl.BlockSpec((gb, 2 * SSM_N, lc), lambda i: (i, 0, 0)),
                  pl.BlockSpec((gb, lc, 2 * SSM_N), lambda i: (i, 0, 0)),
                  pl.BlockSpec((gb, SSM_N, 2), lambda i: (i, 0, 0))],
        out_specs=[pl.BlockSpec((gb, lc, LANES), lambda i: (i, 0, 0)),
                   pl.BlockSpec((gb, 2 * SSM_N, LANES), lambda i: (i, 0, 0))],
        out_shape=[jax.ShapeDtypeStruct((n_g, lc, LANES), jnp.float32),
                   jax.ShapeDtypeStruct((n_g, 2 * SSM_N, LANES), jnp.float32)],
        compiler_params=_params(("parallel",)),
        name="ssm_sample",
    )(ut, h0t, _mxu(m_op), _mxu(p_op), _mxu(q_op), a_op)
    y = y[:, :, :nreq].reshape(n_g, n_new, SSM_C, nreq).transpose(3, 1, 0, 2).reshape(nreq, n_new, n_g * SSM_C)
    y = jnp.pad(y, ((0, 0), (0, SAMPLE_ROWS - n_new), (0, 0))).reshape(nreq * SAMPLE_ROWS, n_g * SSM_C)
    h_new = h[:, :, :nreq].reshape(n_g, 2, SSM_N, nreq).transpose(3, 0, 2, 1)
    return y, h_new


def _mix_kernel(o0_ref, l0_ref, o1_ref, l1_ref, o2_ref, l2_ref, ys_ref, u_ref, ga_ref, gs_ref, dsk_ref, wglu_ref,
                bglu_ref, wap_ref, wsp_ref, o_ref, attn_buf):
    outs = (o0_ref, o1_ref, o2_ref)
    for j in range(HEADS):
        sl = slice(j * HD, (j + 1) * HD)
        lses = [l[:, sl] for l in (l0_ref, l1_ref, l2_ref)]
        top = jnp.maximum(jnp.maximum(lses[0], lses[1]), lses[2])
        ws = [jnp.exp(l - top) for l in lses]
        num = ws[0] * outs[0][:, sl] + ws[1] * outs[1][:, sl] + ws[2] * outs[2][:, sl]
        attn_buf[:, sl] = _mxu(num / (ws[0] + ws[1] + ws[2]))
    y = jax.nn.gelu(ys_ref[...] + dsk_ref[...] * u_ref[...])
    ssm = y * jax.nn.sigmoid(_dot(_mxu(y), wglu_ref[...]) + bglu_ref[...])
    merged = (jax.nn.sigmoid(ga_ref[...]) * _dot(attn_buf[...], wap_ref[...])
              + jax.nn.sigmoid(gs_ref[...]) * _dot(_mxu(ssm), wsp_ref[...]))
    o_ref[...] = _mxu(merged)


def _mix(parts, ys, z, d_skip, w_glu, b_glu, w_ap, w_sp, tm):
    m = ys.shape[0]
    ssm_w = ys.shape[1]
    d_model = w_ap.shape[1]
    u_off, ga_off = 3 * ATTN_W, 3 * ATTN_W + ssm_w
    gs_off = ga_off + d_model
    assert u_off % ssm_w == 0 and ga_off % d_model == 0 and m % tm == 0
    row = lambda w: pl.BlockSpec((tm, w), lambda i: (i, 0))
    const = lambda a: pl.BlockSpec(a.shape, lambda i: (0,) * a.ndim)
    d_skip, b_glu = d_skip.reshape(1, -1), b_glu.reshape(1, -1)
    return pl.pallas_call(
        _mix_kernel,
        grid=(m // tm,),
        in_specs=[row(GROUP_W)] * (2 * N_GROUPS) + [row(ssm_w),
                  pl.BlockSpec((tm, ssm_w), lambda i: (i, u_off // ssm_w)),
                  pl.BlockSpec((tm, d_model), lambda i: (i, ga_off // d_model)),
                  pl.BlockSpec((tm, d_model), lambda i: (i, gs_off // d_model)),
                  const(d_skip), const(w_glu), const(b_glu), const(w_ap), const(w_sp)],
        out_specs=row(d_model),
        out_shape=jax.ShapeDtypeStruct((m, d_model), MXU_DTYPE),
        scratch_shapes=[pltpu.VMEM((tm, GROUP_W), MXU_DTYPE)],
        compiler_params=_params(("parallel",)),
        name="mix",
    )(*parts, ys, z, z, z, d_skip, w_glu, b_glu, w_ap, w_sp)


def _layer_norm(h, g, b):
    mu = jnp.mean(h, axis=-1, keepdims=True)
    d = h - mu
    var = jnp.mean(d * d, axis=-1, keepdims=True)
    return d * lax.rsqrt(var + LN_EPS) * g + b


def _out_kernel(m_ref, x_ref, wo_ref, g_ref, b_ref, rw_ref, rb_ref, x1_ref, x1b_ref, lg_ref, *, alpha):
    h = alpha * x_ref[...] + _dot(m_ref[...], wo_ref[...])
    x1 = _layer_norm(h, g_ref[...], b_ref[...])
    x1_ref[...] = x1
    x1b_ref[...] = _mxu(x1)
    lg_ref[...] = jnp.dot(x1, rw_ref[...], preferred_element_type=jnp.float32,
                          precision=lax.Precision.HIGHEST) + rb_ref[...]


def _out_proj(merged, x, w_o, g, b, rw, rb, alpha, tm):
    m, d_model = x.shape
    assert m % tm == 0
    row = lambda w: pl.BlockSpec((tm, w), lambda i: (i, 0))
    const = lambda a: pl.BlockSpec(a.shape, lambda i: (0,) * a.ndim)
    g, b = g.reshape(1, -1), b.reshape(1, -1)
    return pl.pallas_call(
        functools.partial(_out_kernel, alpha=alpha),
        grid=(m // tm,),
        in_specs=[row(d_model), row(d_model), const(w_o), const(g), const(b), const(rw), const(rb)],
        out_specs=[row(d_model), row(d_model), row(LANES)],
        out_shape=[jax.ShapeDtypeStruct((m, d_model), jnp.float32),
                   jax.ShapeDtypeStruct((m, d_model), MXU_DTYPE),
                   jax.ShapeDtypeStruct((m, LANES), jnp.float32)],
        compiler_params=_params(("parallel",)),
        name="out_proj",
    )(merged, x, w_o, g, b, rw, rb)


def _moe_kernel(ce_ref, cr_ref, cn_ref, tot_ref, x_hbm, wg_ref, wl_ref, wd_ref, bg_ref, bl_ref, bd_ref,
                y_hbm, xbuf, acc, wgb, wlb, wdb, sem_in, sem_out, *, n_ff_tiles, n_sub_total):
    c = pl.program_id(0)
    j = pl.program_id(1)
    nsb = cn_ref[c]
    row0 = cr_ref[c]

    def sub_rows(s):
        return pl.ds(pl.multiple_of(s * MOE_SUB, MOE_SUB), MOE_SUB)

    def x_copy(s):
        return pltpu.make_async_copy(x_hbm.at[sub_rows(row0 + s)], xbuf.at[sub_rows(s)], sem_in)

    def y_copy(s):
        return pltpu.make_async_copy(acc.at[sub_rows(s)], y_hbm.at[sub_rows(row0 + s)], sem_out)

    def for_subs(n, fn):
        def body(s, carry):
            fn(s)
            return carry
        lax.fori_loop(0, n, body, 0)

    @pl.when((j == 0) & (nsb > 0))
    def _():
        for_subs(nsb, lambda s: x_copy(s).start())

        def init(s):
            acc[sub_rows(s), :] = jnp.broadcast_to(bd_ref[...], (MOE_SUB, acc.shape[1]))

        for_subs(nsb, init)
        for_subs(nsb, lambda s: x_copy(s).wait())

    @pl.when(nsb > 0)
    def _():
        wgb[...] = _mxu(wg_ref[...])
        wlb[...] = _mxu(wl_ref[...])
        wdb[...] = _mxu(wd_ref[...])

        def sub(s):
            rows = sub_rows(s)
            xs = xbuf[rows, :]
            glu = jnp.minimum(_dot(xs, wgb[...]) + bg_ref[...], SWIGLU_LIMIT)
            lin = jnp.clip(_dot(xs, wlb[...]) + bl_ref[...], -SWIGLU_LIMIT, SWIGLU_LIMIT)
            act = glu * jax.nn.sigmoid(SWIGLU_ALPHA * glu) * (lin + 1.0)
            acc[rows, :] += _dot(_mxu(act), wdb[...])

        def pair(p):
            sub(2 * p)
            sub(2 * p + 1)

        for_subs(nsb // 2, pair)

        @pl.when(nsb % 2 == 1)
        def _():
            sub(nsb - 1)

    @pl.when((j == n_ff_tiles - 1) & (nsb > 0))
    def _():
        for_subs(nsb, lambda s: y_copy(s).start())
        for_subs(nsb, lambda s: y_copy(s).wait())

    @pl.when((c == pl.num_programs(0) - 1) & (j == n_ff_tiles - 1))
    def _():
        first = tot_ref[0] // MOE_SUB
        acc[0:MOE_SUB, :] = jnp.zeros((MOE_SUB, acc.shape[1]), acc.dtype)

        def z_copy(s):
            return pltpu.make_async_copy(acc.at[pl.ds(0, MOE_SUB)], y_hbm.at[sub_rows(first + s)], sem_out)

        for_subs(n_sub_total - first, lambda s: z_copy(s).start())
        for_subs(n_sub_total - first, lambda s: z_copy(s).wait())


def _moe_experts(x_sorted, chunk_expert, chunk_row, chunk_subs, total_rows, w_gu, b_gu, w_dn, b_dn, layer):
    n_slots, d_model = x_sorted.shape
    d_ff = w_dn.shape[2]
    tf = MOE_FF_TILE
    n_ff_tiles = d_ff // tf
    n_chunks = chunk_expert.shape[0]
    rows = MOE_SUB * MOE_CHUNK_SUBS
    n_exp = w_gu.shape[1]
    b_gu4 = b_gu.reshape(b_gu.shape[0], n_exp, 1, 2 * d_ff)
    b_dn4 = b_dn.reshape(b_dn.shape[0], n_exp, 1, d_model)

    def ff(j, c, cn):
        return jnp.where(cn[c] > 0, j, n_ff_tiles - 1)

    grid_spec = pltpu.PrefetchScalarGridSpec(
        num_scalar_prefetch=4,
        grid=(n_chunks, n_ff_tiles),
        in_specs=[
            pl.BlockSpec(memory_space=pl.ANY),
            pl.BlockSpec((None, None, d_model, tf), lambda c, j, ce, cr, cn, tot: (layer, ce[c], 0, ff(j, c, cn))),
            pl.BlockSpec((None, None, d_model, tf),
                         lambda c, j, ce, cr, cn, tot: (layer, ce[c], 0, n_ff_tiles + ff(j, c, cn))),
            pl.BlockSpec((None, None, tf, d_model), lambda c, j, ce, cr, cn, tot: (layer, ce[c], ff(j, c, cn), 0)),
            pl.BlockSpec((None, None, 1, tf), lambda c, j, ce, cr, cn, tot: (layer, ce[c], 0, ff(j, c, cn))),
            pl.BlockSpec((None, None, 1, tf),
                         lambda c, j, ce, cr, cn, tot: (layer, ce[c], 0, n_ff_tiles + ff(j, c, cn))),
            pl.BlockSpec((None, None, 1, d_model), lambda c, j, ce, cr, cn, tot: (layer, ce[c], 0, 0)),
        ],
        out_specs=pl.BlockSpec(memory_space=pl.ANY),
        scratch_shapes=[
            pltpu.VMEM((rows, d_model), MXU_DTYPE),
            pltpu.VMEM((rows, d_model), jnp.float32),
            pltpu.VMEM((d_model, tf), MXU_DTYPE),
            pltpu.VMEM((d_model, tf), MXU_DTYPE),
            pltpu.VMEM((tf, d_model), MXU_DTYPE),
            pltpu.SemaphoreType.DMA(()),
            pltpu.SemaphoreType.DMA(()),
        ],
    )
    return pl.pallas_call(
        functools.partial(_moe_kernel, n_ff_tiles=n_ff_tiles, n_sub_total=n_slots // MOE_SUB),
        grid_spec=grid_spec,
        out_shape=jax.ShapeDtypeStruct((n_slots, d_model), jnp.float32),
        compiler_params=_params(("arbitrary", "arbitrary")),
        name="moe_experts",
    )(chunk_expert, chunk_row, chunk_subs, total_rows, x_sorted, w_gu, w_gu, w_dn, b_gu4, b_gu4, b_dn4)


def _route(logits, n_exp):
    n_tok = logits.shape[0]
    top_val, top_idx = lax.top_k(logits[:, :n_exp], TOP_K)
    gate = jax.nn.softmax(top_val, axis=-1)
    n_assign = n_tok * TOP_K
    n_slots = -(-(n_assign + n_exp * (MOE_SUB - 1)) // MOE_SUB) * MOE_SUB
    flat_e = top_idx.reshape(-1)
    onehot = (flat_e[:, None] == jnp.arange(n_exp, dtype=flat_e.dtype)[None, :]).astype(jnp.int32)
    before = jnp.cumsum(onehot, axis=0) - onehot
    rank = jnp.take_along_axis(before, flat_e[:, None], axis=1)[:, 0]
    counts = onehot.sum(axis=0)
    padded = (counts + MOE_SUB - 1) // MOE_SUB * MOE_SUB
    pad_end = jnp.cumsum(padded)
    pad_start = pad_end - padded
    dest = (pad_start[flat_e] + rank).astype(jnp.int32)
    slot_tok = jnp.zeros((n_slots,), jnp.int32).at[dest].set(jnp.arange(n_assign, dtype=jnp.int32) // TOP_K)
    rows = MOE_SUB * MOE_CHUNK_SUBS
    n_chunks = n_slots // rows + n_exp + 1
    per_exp = (padded + rows - 1) // rows
    chunk_end = jnp.cumsum(per_exp)
    cid = jnp.arange(n_chunks)
    exp_of = jnp.minimum(jnp.searchsorted(chunk_end, cid, side='right'), n_exp - 1)
    within = cid - (chunk_end - per_exp)[exp_of]
    live = cid < chunk_end[-1]
    row0 = pad_start[exp_of] + within * rows
    subs = jnp.clip(padded[exp_of] - within * rows, 0, rows) // MOE_SUB
    last_exp = exp_of[jnp.maximum(chunk_end[-1] - 1, 0)]
    chunk_expert = jnp.where(live, exp_of, last_exp).astype(jnp.int32)
    chunk_row = (jnp.where(live, row0, 0) // MOE_SUB).astype(jnp.int32)
    chunk_subs = jnp.where(live, subs, 0).astype(jnp.int32)
    total_rows = pad_end[-1:].astype(jnp.int32)
    return gate, dest, slot_tok, chunk_expert, chunk_row, chunk_subs, total_rows


def _combine_kernel(x_ref, y0_ref, y1_ref, y2_ref, y3_ref, gate_ref, g_ref, b_ref, o_ref, ob_ref, *, alpha):
    h = alpha * x_ref[...]
    for k, y_ref in enumerate((y0_ref, y1_ref, y2_ref, y3_ref)):
        h = h + gate_ref[:, k:k + 1] * y_ref[...]
    x2 = _layer_norm(h, g_ref[...], b_ref[...])
    o_ref[...] = x2
    ob_ref[...] = _mxu(x2)


def _combine(x1, yb_k, gate, g, b, alpha, tm):
    m, d_model = x1.shape
    assert m % tm == 0 and len(yb_k) == TOP_K
    row = lambda w: pl.BlockSpec((tm, w), lambda i: (i, 0))
    const = lambda a: pl.BlockSpec(a.shape, lambda i: (0,) * a.ndim)
    g, b = g.reshape(1, -1), b.reshape(1, -1)
    gate = jnp.pad(gate, ((0, 0), (0, LANES - TOP_K)))
    return pl.pallas_call(
        functools.partial(_combine_kernel, alpha=alpha),
        grid=(m // tm,),
        in_specs=[row(d_model)] * (1 + TOP_K) + [row(LANES), const(g), const(b)],
        out_specs=[row(d_model), row(d_model)],
        out_shape=[jax.ShapeDtypeStruct((m, d_model), jnp.float32),
                   jax.ShapeDtypeStruct((m, d_model), MXU_DTYPE)],
        compiler_params=_params(("parallel",)),
        name="combine",
    )(x1, *yb_k, gate, g, b)


def _forward(x_prompt, x_sample, cache_kv_w128, cache_kv_w512, cache_kv_w2048, state_ssm, w_in, w_attn_proj,
             ssm_lambda_re, ssm_lambda_im, ssm_log_dt, ssm_b_re, ssm_b_im, ssm_c_re, ssm_c_im, ssm_d, w_glu,
             b_glu, w_ssm_proj, w_o, ln1_g, ln1_b, router_w, router_b, w_gate_up, b_gate_up, w_down, b_down,
             ln2_g, ln2_b):
    caches = (cache_kv_w128, cache_kv_w512, cache_kv_w2048)
    bsz, seq, d_model = x_prompt.shape
    nreq, n_new, _ = x_sample.shape
    depth = w_in.shape[0]
    n_exp = router_w.shape[2]
    alpha = (2 * depth) ** 0.25
    n_p = bsz * seq
    n_s = nreq * SAMPLE_ROWS
    assert n_new <= SAMPLE_ROWS

    xp = x_prompt.reshape(n_p, d_model)
    xs = jnp.pad(x_sample, ((0, 0), (0, SAMPLE_ROWS - n_new), (0, 0))).reshape(n_s, d_model)
    xpb, xsb = _mxu(xp), _mxu(xs)
    kv_p = [[] for _ in ATTN_PATTERNS]
    kv_s = [[] for _ in ATTN_PATTERNS]
    h_p, h_s = [], []
    tm_in = 1024 if n_p % 1024 == 0 else 256

    for layer in range(depth):
        w_in_l = _mxu(w_in[layer])
        w_glu_l, w_ap_l, w_sp_l, w_o_l = (_mxu(w[layer]) for w in (w_glu, w_attn_proj, w_ssm_proj, w_o))
        rw_l = jnp.pad(router_w[layer], ((0, 0), (0, LANES - n_exp)))
        rb_l = jnp.pad(router_b[layer], (0, LANES - n_exp)).reshape(1, LANES)
        lam = (ssm_lambda_re[layer], ssm_lambda_im[layer], ssm_log_dt[layer], ssm_b_re[layer], ssm_b_im[layer],
               ssm_c_re[layer], ssm_c_im[layer])
        ssm_w = ssm_d.shape[1]
        u_lo = 3 * ATTN_W

        zp = _matmul(xpb, w_in_l, tm_in, 1024)
        z3 = zp.reshape(bsz, seq, -1)
        parts_p = [a for g in range(N_GROUPS) for a in _attn_prompt(z3, g)]
        ut = _u_proj(xpb, w_in_l[:, u_lo:u_lo + ssm_w], ssm_w // SSM_C)
        ys_p, h_last = _ssm_prompt(ut, bsz, seq, _ssm_operators(*lam, SSM_CHUNK))
        h_p.append(h_last)
        for g, (window, _) in enumerate(ATTN_PATTERNS):
            keep = min(window, seq)
            k_g = z3[:, seq - keep:, ATTN_W + g * GROUP_W:ATTN_W + (g + 1) * GROUP_W]
            v_g = z3[:, seq - keep:, 2 * ATTN_W + g * GROUP_W:2 * ATTN_W + (g + 1) * GROUP_W]
            kv_p[g].append(jnp.stack([k_g, v_g], axis=2).reshape(bsz, keep, 2, HEADS, HD))
        mg_p = _mix(parts_p, ys_p, zp, ssm_d[layer], w_glu_l, b_glu[layer], w_ap_l, w_sp_l, 128)
        x1_p, x1b_p, lg_p = _out_proj(mg_p, xp, w_o_l, ln1_g[layer], ln1_b[layer], rw_l, rb_l, alpha, 256)

        zs = _matmul(xsb, w_in_l, n_s, 1024)
        parts_s = _attn_sample(zs, caches, layer, nreq)
        ys_s, h_new = _ssm_sample(zs[:, u_lo:u_lo + ssm_w], state_ssm[layer], nreq, n_new,
                                  _ssm_operators(*lam, n_new))
        h_s.append(h_new)
        zs3 = zs.reshape(nreq, SAMPLE_ROWS, -1)[:, :n_new]
        for g in range(N_GROUPS):
            k_g = zs3[:, :, ATTN_W + g * GROUP_W:ATTN_W + (g + 1) * GROUP_W]
            v_g = zs3[:, :, 2 * ATTN_W + g * GROUP_W:2 * ATTN_W + (g + 1) * GROUP_W]
            new = jnp.stack([k_g, v_g], axis=2).reshape(nreq, n_new, 2, HEADS, HD)
            old = caches[g][layer]
            kv_s[g].append(jnp.concatenate([old.astype(new.dtype), new], axis=1)[:, -old.shape[1]:])
        mg_s = _mix(parts_s, ys_s, zs, ssm_d[layer], w_glu_l, b_glu[layer], w_ap_l, w_sp_l, n_s)
        x1_s, x1b_s, lg_s = _out_proj(mg_s, xs, w_o_l, ln1_g[layer], ln1_b[layer], rw_l, rb_l, alpha, n_s)

        x1b = jnp.concatenate([x1b_p, x1b_s], axis=0)
        logits = jnp.concatenate([lg_p, lg_s], axis=0)
        gate, dest, slot_tok, c_exp, c_row, c_sub, total = _route(logits, n_exp)
        yb = _moe_experts(x1b[slot_tok], c_exp, c_row, c_sub, total, w_gate_up, b_gate_up, w_down, b_down, layer)
        dest = dest.reshape(n_p + n_s, TOP_K)
        xp, xpb = _combine(x1_p, [yb[dest[:n_p, k]] for k in range(TOP_K)], gate[:n_p],
                           ln2_g[layer], ln2_b[layer], alpha, 256)
        xs, xsb = _combine(x1_s, [yb[dest[n_p:, k]] for k in range(TOP_K)], gate[n_p:],
                           ln2_g[layer], ln2_b[layer], alpha, n_s)

    y_prompt = xp.reshape(bsz, seq, d_model)
    y_sample = xs.reshape(nreq, SAMPLE_ROWS, d_model)[:, :n_new]
    return (y_prompt, y_sample,
            jnp.stack(kv_p[0]), jnp.stack(kv_p[1]), jnp.stack(kv_p[2]), jnp.stack(h_p),
            jnp.stack(kv_s[0]), jnp.stack(kv_s[1]), jnp.stack(kv_s[2]), jnp.stack(h_s))


_forward_jit = jax.jit(_forward)


def kernel(x_prompt, x_sample, cache_kv_w128, cache_kv_w512, cache_kv_w2048, state_ssm, w_in, w_attn_proj,
           ssm_lambda_re, ssm_lambda_im, ssm_log_dt, ssm_b_re, ssm_b_im, ssm_c_re, ssm_c_im, ssm_d, w_glu, b_glu,
           w_ssm_proj, w_o, ln1_g, ln1_b, router_w, router_b, w_gate_up, b_gate_up, w_down, b_down, ln2_g, ln2_b):
    return _forward_jit(x_prompt, x_sample, cache_kv_w128, cache_kv_w512, cache_kv_w2048, state_ssm, w_in,
                        w_attn_proj, ssm_lambda_re, ssm_lambda_im, ssm_log_dt, ssm_b_re, ssm_b_im, ssm_c_re,
                        ssm_c_im, ssm_d, w_glu, b_glu, w_ssm_proj, w_o, ln1_g, ln1_b, router_w, router_b,
                        w_gate_up, b_gate_up, w_down, b_down, ln2_g, ln2_b)
```

```python
import functools
import math

import jax
import jax.numpy as jnp
from jax import lax
from jax.experimental import pallas as pl
from jax.experimental.pallas import tpu as pltpu

ATTN_PATTERNS = ((128, 1), (512, 4), (2048, 16))
N_GROUPS = len(ATTN_PATTERNS)
HEADS = 4
HD = 128
GROUP_W = HEADS * HD
ATTN_W = N_GROUPS * GROUP_W
N_HEADS = N_GROUPS * HEADS
ALIBI_SLOPES = tuple(2.0 ** (-8.0 * (h + 1) / N_HEADS) for h in range(N_HEADS))
QB = 128
ATTN_BLOCK = QB * max(d for _, d in ATTN_PATTERNS)
SSM_C = 16
SSM_N = 64
TOP_K = 4
SWIGLU_LIMIT = 7.0
SWIGLU_ALPHA = 1.702
LN_EPS = 1e-5

LANES = 128
SUBLANES = 8
VMEM_LIMIT_BYTES = 56 * 1024 * 1024

SSM_CHUNK = 16
SSM_GROUP_BLOCK = 8
SAMPLE_ROWS = 16
MOE_SUB = 256
MOE_CHUNK_SUBS = 8
MOE_FF_TILE = 256
MOE_ROW_UNROLL = 8
MXU_DTYPE = jnp.bfloat16
NEG_BIG = -1e30


def _params(sem):
    return pltpu.CompilerParams(dimension_semantics=sem, vmem_limit_bytes=VMEM_LIMIT_BYTES)


def _mxu(a):
    return a.astype(MXU_DTYPE)


def _dot(a, b):
    return jnp.dot(a, b, preferred_element_type=jnp.float32)


def _mm_kernel(x_ref, w_ref, o_ref):
    o_ref[...] = _dot(_mxu(x_ref[...]), w_ref[...])


def _matmul(x, w, tm, tn):
    m, k = x.shape
    n = w.shape[1]
    assert m % tm == 0 and n % tn == 0
    return pl.pallas_call(
        _mm_kernel,
        grid=(m // tm, n // tn),
        in_specs=[pl.BlockSpec((tm, k), lambda i, j: (i, 0)),
                  pl.BlockSpec((k, tn), lambda i, j: (0, j))],
        out_specs=pl.BlockSpec((tm, tn), lambda i, j: (i, j)),
        out_shape=jax.ShapeDtypeStruct((m, n), jnp.float32),
        compiler_params=_params(("parallel", "arbitrary")),
        name="in_proj",
    )(x, w)


def _attn_prompt_kernel(q_ref, kc_ref, kp_ref, vc_ref, vp_ref, o_ref, l_ref, kf, vf, *, tile, slopes, dilation):
    h = pl.program_id(1)
    n = pl.program_id(2)
    slope = jnp.float32(slopes[HEADS - 1])
    for hh in range(HEADS - 2, -1, -1):
        slope = jnp.where(h == hh, jnp.float32(slopes[hh]), slope)
    a = lax.broadcasted_iota(jnp.int32, (QB, 2 * QB), 0)
    c = lax.broadcasted_iota(jnp.int32, (QB, 2 * QB), 1)
    rel = a + QB - c
    band = (rel >= 0) & (rel <= QB)
    first_lo = jnp.where(n > 0, 0, QB)
    bias = slope * (rel * dilation).astype(jnp.float32)
    scale = HD ** -0.5

    def rows(r, start, count):
        if dilation == 1:
            return pl.ds(start, count)
        return pl.ds(start * dilation + r, count, stride=dilation)

    for r in range(dilation):
        kf[0:QB, :] = _mxu(kp_ref[rows(r, 0, QB), :])
        kf[QB:, :] = _mxu(kc_ref[rows(r, 0, tile), :])
        vf[0:QB, :] = _mxu(vp_ref[rows(r, 0, QB), :])
        vf[QB:, :] = _mxu(vc_ref[rows(r, 0, tile), :])
        for i in range(tile // QB):
            valid = (band & (c >= first_lo)) if i == 0 else band
            q = _mxu(q_ref[rows(r, i * QB, QB), :])
            k = kf[i * QB:(i + 2) * QB, :]
            v = vf[i * QB:(i + 2) * QB, :]
            s = lax.dot_general(q, k, (((1,), (1,)), ((), ())), preferred_element_type=jnp.float32) * scale
            s = jnp.where(valid, s - bias, NEG_BIG)
            m = jnp.max(s, axis=-1, keepdims=True)
            p = jnp.exp(s - m)
            l = jnp.sum(p, axis=-1, keepdims=True)
            o_ref[rows(r, i * QB, QB), :] = _dot(_mxu(p), v) / l
            l_ref[rows(r, i * QB, QB), :] = jnp.broadcast_to(m + jnp.log(l), (QB, HD))


def _attn_prompt(z3, g):
    window, dilation = ATTN_PATTERNS[g]
    assert window // dilation == QB
    bsz, seq, zc = z3.shape
    blk = ATTN_BLOCK
    tile = blk // dilation
    assert seq % blk == 0 and zc % HD == 0
    qcol, kcol, vcol = g * HEADS, (ATTN_W // HD) + g * HEADS, 2 * (ATTN_W // HD) + g * HEADS
    sub = blk // (QB * dilation)

    def cur(col):
        return pl.BlockSpec((None, blk, HD), lambda b, h, n: (b, n, col + h))

    def prev(col):
        return pl.BlockSpec((None, QB * dilation, HD), lambda b, h, n: (b, jnp.maximum(n * sub - 1, 0), col + h))

    slopes = ALIBI_SLOPES[g * HEADS:(g + 1) * HEADS]
    out_spec = pl.BlockSpec((None, blk, HD), lambda b, h, n: (b, n, h))
    out_shape = jax.ShapeDtypeStruct((bsz, seq, GROUP_W), jnp.float32)
    o, lse = pl.pallas_call(
        functools.partial(_attn_prompt_kernel, tile=tile, slopes=slopes, dilation=dilation),
        grid=(bsz, HEADS, seq // blk),
        in_specs=[cur(qcol), cur(kcol), prev(kcol), cur(vcol), prev(vcol)],
        out_specs=[out_spec, out_spec],
        out_shape=[out_shape, out_shape],
        scratch_shapes=[pltpu.VMEM((tile + QB, HD), MXU_DTYPE),
                        pltpu.VMEM((tile + QB, HD), MXU_DTYPE)],
        compiler_params=_params(("parallel", "parallel", "arbitrary")),
        name="attn_prompt_g%d" % g,
    )(z3, z3, z3, z3, z3)
    return o.reshape(bsz * seq, GROUP_W), lse.reshape(bsz * seq, GROUP_W)


def _attn_sample_kernel(z_ref, c0_ref, c1_ref, c2_ref, o0_ref, l0_ref, o1_ref, l1_ref, o2_ref, l2_ref):
    rows = SAMPLE_ROWS
    scale = HD ** -0.5
    for g, (c_ref, o_ref, l_ref) in enumerate(((c0_ref, o0_ref, l0_ref), (c1_ref, o1_ref, l1_ref),
                                                (c2_ref, o2_ref, l2_ref))):
        window, dilation = ATTN_PATTERNS[g]
        n_past = c_ref.shape[0]
        s_idx = lax.broadcasted_iota(jnp.int32, (rows, n_past), 0)
        i_idx = lax.broadcasted_iota(jnp.int32, (rows, n_past), 1)
        rel_c = n_past + s_idx - i_idx
        valid_c = (rel_c <= window) & ((rel_c & (dilation - 1)) == 0)
        dist_c = rel_c.astype(jnp.float32)
        s2 = lax.broadcasted_iota(jnp.int32, (rows, rows), 0)
        t2 = lax.broadcasted_iota(jnp.int32, (rows, rows), 1)
        rel_n = s2 - t2
        valid_n = (rel_n >= 0) & ((rel_n & (dilation - 1)) == 0)
        dist_n = rel_n.astype(jnp.float32)
        for h in range(HEADS):
            slope = ALIBI_SLOPES[g * HEADS + h]
            col = g * GROUP_W + h * HD
            q = _mxu(z_ref[:, col:col + HD])
            kn = _mxu(z_ref[:, ATTN_W + col:ATTN_W + col + HD])
            vn = _mxu(z_ref[:, 2 * ATTN_W + col:2 * ATTN_W + col + HD])
            kc = _mxu(c_ref[:, h * HD:(h + 1) * HD])
            vc = _mxu(c_ref[:, GROUP_W + h * HD:GROUP_W + (h + 1) * HD])
            sc = lax.dot_general(q, kc, (((1,), (1,)), ((), ())), preferred_element_type=jnp.float32) * scale
            sn = lax.dot_general(q, kn, (((1,), (1,)), ((), ())), preferred_element_type=jnp.float32) * scale
            sc = jnp.where(valid_c, sc - slope * dist_c, NEG_BIG)
            sn = jnp.where(valid_n, sn - slope * dist_n, NEG_BIG)
            m = jnp.maximum(jnp.max(sc, axis=-1, keepdims=True), jnp.max(sn, axis=-1, keepdims=True))
            pc = jnp.exp(sc - m)
            pn = jnp.exp(sn - m)
            l = jnp.sum(pc, axis=-1, keepdims=True) + jnp.sum(pn, axis=-1, keepdims=True)
            o = (_dot(_mxu(pc), vc) + _dot(_mxu(pn), vn)) / l
            o_ref[:, h * HD:(h + 1) * HD] = o
            l_ref[:, h * HD:(h + 1) * HD] = jnp.broadcast_to(m + jnp.log(l), (rows, HD))


def _attn_sample(zs, caches, layer, nreq):
    zc = zs.shape[1]
    z3 = zs.reshape(nreq, SAMPLE_ROWS, zc)
    cache_specs, cache_args = [], []
    for cch in caches:
        depth, nb, n_past = cch.shape[:3]
        cache_args.append(cch.reshape(depth, nb, n_past, 2 * GROUP_W))
        cache_specs.append(pl.BlockSpec((None, None, n_past, 2 * GROUP_W), lambda b: (layer, b, 0, 0)))
    out_spec = pl.BlockSpec((None, SAMPLE_ROWS, GROUP_W), lambda b: (b, 0, 0))
    out_shape = jax.ShapeDtypeStruct((nreq, SAMPLE_ROWS, GROUP_W), jnp.float32)
    outs = pl.pallas_call(
        _attn_sample_kernel,
        grid=(nreq,),
        in_specs=[pl.BlockSpec((None, SAMPLE_ROWS, zc), lambda b: (b, 0, 0))] + cache_specs,
        out_specs=[out_spec] * (2 * N_GROUPS),
        out_shape=[out_shape] * (2 * N_GROUPS),
        compiler_params=_params(("parallel",)),
        name="attn_sample",
    )(z3, *cache_args)
    return [o.reshape(nreq * SAMPLE_ROWS, GROUP_W) for o in outs]


def _ssm_operators(lam_re, lam_im, log_dt, b_re, b_im, c_re, c_im, chunk):
    f32 = jnp.float32
    hi = lax.Precision.HIGHEST
    lam_re, lam_im = lam_re.astype(f32), lam_im.astype(f32)
    dt = jnp.exp(log_dt.astype(f32))[:, None]
    decay = jnp.exp(lam_re * dt)
    a_re = decay * jnp.cos(lam_im * dt)
    a_im = decay * jnp.sin(lam_im * dt)
    den = lam_re * lam_re + lam_im * lam_im
    f_re = ((a_re - 1.0) * lam_re + a_im * lam_im) / den
    f_im = (a_im * lam_re - (a_re - 1.0) * lam_im) / den
    b_re, b_im = b_re.astype(f32), b_im.astype(f32)
    bb_re = f_re[..., None] * b_re - f_im[..., None] * b_im
    bb_im = f_re[..., None] * b_im + f_im[..., None] * b_re
    c_re, c_im = c_re.astype(f32), c_im.astype(f32)

    def power(p):
        p = jnp.asarray(p, f32)[..., None, None]
        mag = jnp.exp(lam_re * dt * p)
        ang = lam_im * dt * p
        return mag * jnp.cos(ang), mag * jnp.sin(ang)

    n_g = lam_re.shape[0]
    ap_re, ap_im = power(jnp.arange(chunk + 1))
    abb_re = ap_re[:chunk, :, :, None] * bb_re - ap_im[:chunk, :, :, None] * bb_im
    abb_im = ap_re[:chunk, :, :, None] * bb_im + ap_im[:chunk, :, :, None] * bb_re
    kern = (jnp.einsum('gcn,tgnd->tgcd', c_re, abb_re, precision=hi)
            - jnp.einsum('gcn,tgnd->tgcd', c_im, abb_im, precision=hi))
    kern = jnp.concatenate([kern, jnp.zeros_like(kern[:1])], axis=0)
    lag = jnp.arange(chunk)[:, None] - jnp.arange(chunk)[None, :]
    m_op = kern[jnp.where(lag >= 0, lag, chunk)]
    m_op = m_op.transpose(2, 0, 3, 1, 4).reshape(n_g, chunk * SSM_C, chunk * SSM_C)
    rev = chunk - 1 - jnp.arange(chunk)
    p_re = abb_re[rev].transpose(1, 2, 0, 3).reshape(n_g, SSM_N, chunk * SSM_C)
    p_im = abb_im[rev].transpose(1, 2, 0, 3).reshape(n_g, SSM_N, chunk * SSM_C)
    p_op = jnp.concatenate([p_re, p_im], axis=1)
    a1_re, a1_im = ap_re[1:], ap_im[1:]
    q_re = c_re[None] * a1_re[:, :, None, :] - c_im[None] * a1_im[:, :, None, :]
    q_im = -(c_re[None] * a1_im[:, :, None, :] + c_im[None] * a1_re[:, :, None, :])
    q_op = jnp.concatenate([q_re, q_im], axis=-1).transpose(1, 0, 2, 3).reshape(n_g, chunk * SSM_C, 2 * SSM_N)
    return m_op, p_op, q_op, power


def _u_proj_kernel(x_ref, w_ref, u_ref, slab):
    n_slab, n_tok, _ = slab.shape
    n_chunk = n_tok // SSM_CHUNK
    groups_per_slab = LANES // SSM_C
    for j2 in range(n_slab // 2):
        zt = _dot(_mxu(x_ref[...]), w_ref[:, 2 * j2 * LANES:2 * (j2 + 1) * LANES])
        slab[2 * j2] = zt[:, :LANES]
        slab[2 * j2 + 1] = zt[:, LANES:]
    for j in range(n_slab):
        for s in range(SSM_CHUNK):
            piece = slab[j, pl.ds(s, n_chunk, stride=SSM_CHUNK), :].T
            for gg in range(groups_per_slab):
                u_ref[j * groups_per_slab + gg, s * SSM_C:(s + 1) * SSM_C, :] = _mxu(piece[gg * SSM_C:(gg + 1) * SSM_C, :])


def _u_proj(x, w_u, n_g):
    n_tok, k = x.shape
    blk_tok = LANES * SSM_CHUNK
    halves = 2
    cols = w_u.shape[1] // halves
    assert n_tok % blk_tok == 0 and cols % (2 * LANES) == 0 and w_u.shape[1] == n_g * SSM_C
    lc = SSM_CHUNK * SSM_C
    return pl.pallas_call(
        _u_proj_kernel,
        grid=(n_tok // blk_tok, halves),
        in_specs=[pl.BlockSpec((blk_tok, k), lambda i, hf: (i, 0)),
                  pl.BlockSpec((k, cols), lambda i, hf: (0, hf))],
        out_specs=pl.BlockSpec((n_g // halves, lc, LANES), lambda i, hf: (hf, 0, i)),
        out_shape=jax.ShapeDtypeStruct((n_g, lc, n_tok // SSM_CHUNK), MXU_DTYPE),
        scratch_shapes=[pltpu.VMEM((cols // LANES, blk_tok, LANES), jnp.float32)],
        compiler_params=_params(("parallel", "arbitrary")),
        name="u_proj",
    )(x, w_u)


def _ssm_prompt_kernel(u_ref, m_ref, p_ref, q_ref, pw_ref, y_ref, h_ref, ybuf, *, chunks_per_seq, n_steps):
    n_cols = u_ref.shape[2]
    lane = lax.broadcasted_iota(jnp.int32, (SSM_N, n_cols), 1) % chunks_per_seq
    for gi in range(u_ref.shape[0]):
        u = u_ref[gi]
        xc = _dot(p_ref[gi], u)
        h_re, h_im = xc[:SSM_N], xc[SSM_N:]
        for j in range(n_steps):
            sh = 1 << j
            a_re = pw_ref[gi, :, j:j + 1]
            a_im = pw_ref[gi, :, n_steps + j:n_steps + j + 1]
            keep = lane >= sh
            s_re = jnp.where(keep, pltpu.roll(h_re, sh, 1), 0.0)
            s_im = jnp.where(keep, pltpu.roll(h_im, sh, 1), 0.0)
            h_re, h_im = h_re + a_re * s_re - a_im * s_im, h_im + a_re * s_im + a_im * s_re
        h_ref[gi, 0:SSM_N, :] = h_re
        h_ref[gi, SSM_N:, :] = h_im
        first = lane >= 1
        hp = jnp.concatenate([jnp.where(first, pltpu.roll(h_re, 1, 1), 0.0),
                              jnp.where(first, pltpu.roll(h_im, 1, 1), 0.0)], axis=0)
        ybuf[gi] = _dot(m_ref[gi], u) + _dot(q_ref[gi], _mxu(hp))
    for t in range(SSM_CHUNK):
        rows = jnp.concatenate([ybuf[gi, t * SSM_C:(t + 1) * SSM_C, :] for gi in range(u_ref.shape[0])], axis=0)
        for cb in range(n_cols // LANES):
            y_ref[pl.ds(cb * LANES * SSM_CHUNK + t, LANES, stride=SSM_CHUNK), :] = rows[:, cb * LANES:(cb + 1) * LANES].T


def _ssm_prompt(ut, bsz, seq, ops):
    m_op, p_op, q_op, power = ops
    n_g = m_op.shape[0]
    chunks_per_seq = seq // SSM_CHUNK
    n_cols = bsz * chunks_per_seq
    n_steps = int(math.log2(chunks_per_seq))
    assert (1 << n_steps) == chunks_per_seq and n_cols % LANES == 0
    lc = SSM_CHUNK * SSM_C
    pw_re, pw_im = power(SSM_CHUNK * (2 ** jnp.arange(n_steps)))
    pw = jnp.concatenate([pw_re, pw_im], axis=0).transpose(1, 2, 0)
    gb = LANES // SSM_C
    assert n_g % gb == 0
    y, h = pl.pallas_call(
        functools.partial(_ssm_prompt_kernel, chunks_per_seq=chunks_per_seq, n_steps=n_steps),
        grid=(n_g // gb,),
        in_specs=[pl.BlockSpec((gb, lc, n_cols), lambda i: (i, 0, 0)),
                  pl.BlockSpec((gb, lc, lc), lambda i: (i, 0, 0)),
                  pl.BlockSpec((gb, 2 * SSM_N, lc), lambda i: (i, 0, 0)),
                  pl.BlockSpec((gb, lc, 2 * SSM_N), lambda i: (i, 0, 0)),
                  pl.BlockSpec((gb, SSM_N, 2 * n_steps), lambda i: (i, 0, 0))],
        out_specs=[pl.BlockSpec((bsz * seq, LANES), lambda i: (0, i)),
                   pl.BlockSpec((gb, 2 * SSM_N, n_cols), lambda i: (i, 0, 0))],
        out_shape=[jax.ShapeDtypeStruct((bsz * seq, n_g * SSM_C), jnp.float32),
                   jax.ShapeDtypeStruct((n_g, 2 * SSM_N, n_cols), jnp.float32)],
        scratch_shapes=[pltpu.VMEM((gb, lc, n_cols), jnp.float32)],
        compiler_params=_params(("parallel",)),
        name="ssm_prompt",
    )(ut, _mxu(m_op), _mxu(p_op), _mxu(q_op), pw)
    h_last = h.reshape(n_g, 2, SSM_N, bsz, chunks_per_seq)[..., -1].transpose(3, 0, 2, 1)
    return y, h_last


def _ssm_sample_kernel(u_ref, h0_ref, m_ref, p_ref, q_ref, a_ref, y_ref, h_ref):
    for gi in range(u_ref.shape[0]):
        u = u_ref[gi]
        h0 = h0_ref[gi]
        h0_re, h0_im = h0[:SSM_N], h0[SSM_N:]
        a_re = a_ref[gi, :, 0:1]
        a_im = a_ref[gi, :, 1:2]
        xc = _dot(p_ref[gi], u)
        h_ref[gi, 0:SSM_N, :] = a_re * h0_re - a_im * h0_im + xc[:SSM_N]
        h_ref[gi, SSM_N:, :] = a_re * h0_im + a_im * h0_re + xc[SSM_N:]
        y_ref[gi] = _dot(m_ref[gi], u) + _dot(q_ref[gi], _mxu(h0))


def _ssm_sample(u, h0, nreq, n_new, ops):
    m_op, p_op, q_op, power = ops
    n_g = m_op.shape[0]
    lc = n_new * SSM_C
    ut = u.reshape(nreq, SAMPLE_ROWS, n_g, SSM_C)[:, :n_new].transpose(2, 1, 3, 0).reshape(n_g, lc, nreq)
    ut = _mxu(jnp.pad(ut, ((0, 0), (0, 0), (0, LANES - nreq))))
    h0t = h0.astype(jnp.float32).transpose(1, 3, 2, 0).reshape(n_g, 2 * SSM_N, nreq)
    h0t = jnp.pad(h0t, ((0, 0), (0, 0), (0, LANES - nreq)))
    a_re, a_im = power(n_new)
    a_op = jnp.stack([a_re, a_im], axis=-1)
    gb = SSM_GROUP_BLOCK
    y, h = pl.pallas_call(
        _ssm_sample_kernel,
        grid=(n_g // gb,),
        in_specs=[pl.BlockSpec((gb, lc, LANES), lambda i: (i, 0, 0)),
                  pl.BlockSpec((gb, 2 * SSM_N, LANES), lambda i: (i, 0, 0)),
                  pl.BlockSpec((gb, lc, lc), lambda i: (i, 0, 0)),
                  pl.BlockSpec((gb, 2 * SSM_N, lc), lambda i: (i, 0, 0)),
                  pl.BlockSpec((gb, lc, 2 * SSM_N), lambda i: (i, 0, 0)),
                  pl.BlockSpec((gb, SSM_N, 2), lambda i: (i, 0, 0))],
        out_specs=[pl.BlockSpec((gb, lc, LANES), lambda i: (i, 0, 0)),
                   pl.BlockSpec((gb, 2 * SSM_N, LANES), lambda i: (i, 0, 0))],
        out_shape=[jax.ShapeDtypeStruct((n_g, lc, LANES), jnp.float32),
                   jax.ShapeDtypeStruct((n_g, 2 * SSM_N, LANES), jnp.float32)],
        compiler_params=_params(("parallel",)),
        name="ssm_sample",
    )(ut, h0t, _mxu(m_op), _mxu(p_op), _mxu(q_op), a_op)
    y = y[:, :, :nreq].reshape(n_g, n_new, SSM_C, nreq).transpose(3, 1, 0, 2).reshape(nreq, n_new, n_g * SSM_C)
    y = jnp.pad(y, ((0, 0), (0, SAMPLE_ROWS - n_new), (0, 0))).reshape(nreq * SAMPLE_ROWS, n_g * SSM_C)
    h_new = h[:, :, :nreq].reshape(n_g, 2, SSM_N, nreq).transpose(3, 0, 2, 1)
    return y, h_new


def _mix_kernel(o0_ref, l0_ref, o1_ref, l1_ref, o2_ref, l2_ref, ys_ref, u_ref, ga_ref, gs_ref, dsk_ref, wglu_ref,
                bglu_ref, wap_ref, wsp_ref, o_ref, attn_buf):
    outs = (o0_ref, o1_ref, o2_ref)
    for j in range(HEADS):
        sl = slice(j * HD, (j + 1) * HD)
        lses = [l[:, sl] for l in (l0_ref, l1_ref, l2_ref)]
        top = jnp.maximum(jnp.maximum(lses[0], lses[1]), lses[2])
        ws = [jnp.exp(l - top) for l in lses]
        num = ws[0] * outs[0][:, sl] + ws[1] * outs[1][:, sl] + ws[2] * outs[2][:, sl]
        attn_buf[:, sl] = _mxu(num / (ws[0] + ws[1] + ws[2]))
    y = jax.nn.gelu(ys_ref[...] + dsk_ref[...] * u_ref[...])
    ssm = y * jax.nn.sigmoid(_dot(_mxu(y), wglu_ref[...]) + bglu_ref[...])
    merged = (jax.nn.sigmoid(ga_ref[...]) * _dot(attn_buf[...], wap_ref[...])
              + jax.nn.sigmoid(gs_ref[...]) * _dot(_mxu(ssm), wsp_ref[...]))
    o_ref[...] = _mxu(merged)


def _mix(parts, ys, z, d_skip, w_glu, b_glu, w_ap, w_sp, tm):
    m = ys.shape[0]
    ssm_w = ys.shape[1]
    d_model = w_ap.shape[1]
    u_off, ga_off = 3 * ATTN_W, 3 * ATTN_W + ssm_w
    gs_off = ga_off + d_model
    assert u_off % ssm_w == 0 and ga_off % d_model == 0 and m % tm == 0
    row = lambda w: pl.BlockSpec((tm, w), lambda i: (i, 0))
    const = lambda a: pl.BlockSpec(a.shape, lambda i: (0,) * a.ndim)
    d_skip, b_glu = d_skip.reshape(1, -1), b_glu.reshape(1, -1)
    return pl.pallas_call(
        _mix_kernel,
        grid=(m // tm,),
        in_specs=[row(GROUP_W)] * (2 * N_GROUPS) + [row(ssm_w),
                  pl.BlockSpec((tm, ssm_w), lambda i: (i, u_off // ssm_w)),
                  pl.BlockSpec((tm, d_model), lambda i: (i, ga_off // d_model)),
                  pl.BlockSpec((tm, d_model), lambda i: (i, gs_off // d_model)),
                  const(d_skip), const(w_glu), const(b_glu), const(w_ap), const(w_sp)],
        out_specs=row(d_model),
        out_shape=jax.ShapeDtypeStruct((m, d_model), MXU_DTYPE),
        scratch_shapes=[pltpu.VMEM((tm, GROUP_W), MXU_DTYPE)],
        compiler_params=_params(("parallel",)),
        name="mix",
    )(*parts, ys, z, z, z, d_skip, w_glu, b_glu, w_ap, w_sp)


def _layer_norm(h, g, b):
    mu = jnp.mean(h, axis=-1, keepdims=True)
    d = h - mu
    var = jnp.mean(d * d, axis=-1, keepdims=True)
    return d * lax.rsqrt(var + LN_EPS) * g + b


def _split_hi_lo(a):
    hi = _mxu(a)
    return hi, _mxu(a - hi.astype(jnp.float32))


def _out_kernel(m_ref, x_ref, wo_ref, g_ref, b_ref, rw_ref, rb_ref, x1_ref, lg_ref, *, alpha):
    h = alpha * x_ref[...] + _dot(m_ref[...], wo_ref[...])
    x1 = _layer_norm(h, g_ref[...], b_ref[...])
    x1_ref[...] = x1
    hi, lo = _split_hi_lo(x1)
    both = _dot(hi, rw_ref[...])
    lg_ref[...] = both[:, :LANES] + both[:, LANES:] + _dot(lo, rw_ref[:, :LANES]) + rb_ref[...]


def _out_proj(merged, x, w_o, g, b, rw, rb, alpha, tm):
    m, d_model = x.shape
    assert m % tm == 0
    row = lambda w: pl.BlockSpec((tm, w), lambda i: (i, 0))
    const = lambda a: pl.BlockSpec(a.shape, lambda i: (0,) * a.ndim)
    g, b = g.reshape(1, -1), b.reshape(1, -1)
    return pl.pallas_call(
        functools.partial(_out_kernel, alpha=alpha),
        grid=(m // tm,),
        in_specs=[row(d_model), row(d_model), const(w_o), const(g), const(b), const(rw), const(rb)],
        out_specs=[row(d_model), row(LANES)],
        out_shape=[jax.ShapeDtypeStruct((m, d_model), jnp.float32),
                   jax.ShapeDtypeStruct((m, LANES), jnp.float32)],
        compiler_params=_params(("parallel",)),
        name="out_proj",
    )(merged, x, w_o, g, b, rw, rb)


def _moe_kernel(ce_ref, cr_ref, cn_ref, ct_ref, cm_ref, tot_ref, tok_ref, x_hbm, wg_ref, wl_ref, wd_ref, bg_ref,
                bl_ref, bd_ref, y_hbm, xbuf, acc, wgb, wlb, wdb, sem_in, sem_out, *, n_ff_tiles, n_sub_total):
    c = pl.program_id(0)
    j = pl.program_id(1)
    nsb = cn_ref[c]
    row0 = cr_ref[c]
    tok0 = ct_ref[c]
    last_real = cm_ref[c] - 1

    def sub_rows(s):
        return pl.ds(pl.multiple_of(s * MOE_SUB, MOE_SUB), MOE_SUB)

    def x_copy(i):
        tok = tok_ref[tok0 + jnp.minimum(i, last_real)]
        return pltpu.make_async_copy(x_hbm.at[pl.ds(tok, 1)], xbuf.at[pl.ds(i, 1)], sem_in)

    def y_copy(s):
        return pltpu.make_async_copy(acc.at[sub_rows(s)], y_hbm.at[sub_rows(row0 + s)], sem_out)

    def for_subs(n, fn):
        def body(s, carry):
            fn(s)
            return carry
        lax.fori_loop(0, n, body, 0)

    def for_rows(n_sub, fn):
        def body(g, carry):
            for k in range(MOE_ROW_UNROLL):
                fn(g * MOE_ROW_UNROLL + k)
            return carry
        lax.fori_loop(0, n_sub * (MOE_SUB // MOE_ROW_UNROLL), body, 0)

    @pl.when((j == 0) & (nsb > 0))
    def _():
        for_rows(nsb, lambda i: x_copy(i).start())

        def init(s):
            acc[sub_rows(s), :] = jnp.broadcast_to(bd_ref[...], (MOE_SUB, acc.shape[1]))

        for_subs(nsb, init)
        for_rows(nsb, lambda i: x_copy(i).wait())

    @pl.when(nsb > 0)
    def _():
        wgb[...] = _mxu(wg_ref[...])
        wlb[...] = _mxu(wl_ref[...])
        wdb[...] = _mxu(wd_ref[...])

        def sub(s):
            rows = sub_rows(s)
            xs = _mxu(xbuf[rows, :])
            glu = jnp.minimum(_dot(xs, wgb[...]) + bg_ref[...], SWIGLU_LIMIT)
            lin = jnp.clip(_dot(xs, wlb[...]) + bl_ref[...], -SWIGLU_LIMIT, SWIGLU_LIMIT)
            act = glu * jax.nn.sigmoid(SWIGLU_ALPHA * glu) * (lin + 1.0)
            acc[rows, :] += _dot(_mxu(act), wdb[...])

        def pair(p):
            sub(2 * p)
            sub(2 * p + 1)

        for_subs(nsb // 2, pair)

        @pl.when(nsb % 2 == 1)
        def _():
            sub(nsb - 1)

    @pl.when((j == n_ff_tiles - 1) & (nsb > 0))
    def _():
        for_subs(nsb, lambda s: y_copy(s).start())
        for_subs(nsb, lambda s: y_copy(s).wait())

    @pl.when((c == pl.num_programs(0) - 1) & (j == n_ff_tiles - 1))
    def _():
        first = tot_ref[0] // MOE_SUB
        acc[0:MOE_SUB, :] = jnp.zeros((MOE_SUB, acc.shape[1]), acc.dtype)

        def z_copy(s):
            return pltpu.make_async_copy(acc.at[pl.ds(0, MOE_SUB)], y_hbm.at[sub_rows(first + s)], sem_out)

        for_subs(n_sub_total - first, lambda s: z_copy(s).start())
        for_subs(n_sub_total - first, lambda s: z_copy(s).wait())


def _moe_experts(x, route, w_gu, b_gu, w_dn, b_dn, layer):
    chunk_expert = route["chunk_expert"]
    n_slots = route["n_slots"]
    d_model = x.shape[1]
    d_ff = w_dn.shape[2]
    tf = MOE_FF_TILE
    n_ff_tiles = d_ff // tf
    n_chunks = chunk_expert.shape[0]
    rows = MOE_SUB * MOE_CHUNK_SUBS
    n_exp = w_gu.shape[1]
    b_gu4 = b_gu.reshape(b_gu.shape[0], n_exp, 1, 2 * d_ff)
    b_dn4 = b_dn.reshape(b_dn.shape[0], n_exp, 1, d_model)

    def ff(j, c, cn):
        return jnp.where(cn[c] > 0, j, n_ff_tiles - 1)

    def wspec(shape, index):
        return pl.BlockSpec((None, None) + shape,
                            lambda c, j, ce, cr, cn, *_: (layer, ce[c]) + index(ff(j, c, cn)))

    grid_spec = pltpu.PrefetchScalarGridSpec(
        num_scalar_prefetch=7,
        grid=(n_chunks, n_ff_tiles),
        in_specs=[
            pl.BlockSpec(memory_space=pl.ANY),
            wspec((d_model, tf), lambda t: (0, t)),
            wspec((d_model, tf), lambda t: (0, n_ff_tiles + t)),
            wspec((tf, d_model), lambda t: (t, 0)),
            wspec((1, tf), lambda t: (0, t)),
            wspec((1, tf), lambda t: (0, n_ff_tiles + t)),
            wspec((1, d_model), lambda t: (0, 0)),
        ],
        out_specs=pl.BlockSpec(memory_space=pl.ANY),
        scratch_shapes=[
            pltpu.VMEM((rows, d_model), jnp.float32),
            pltpu.VMEM((rows, d_model), jnp.float32),
            pltpu.VMEM((d_model, tf), MXU_DTYPE),
            pltpu.VMEM((d_model, tf), MXU_DTYPE),
            pltpu.VMEM((tf, d_model), MXU_DTYPE),
            pltpu.SemaphoreType.DMA(()),
            pltpu.SemaphoreType.DMA(()),
        ],
    )
    return pl.pallas_call(
        functools.partial(_moe_kernel, n_ff_tiles=n_ff_tiles, n_sub_total=n_slots // MOE_SUB),
        grid_spec=grid_spec,
        out_shape=jax.ShapeDtypeStruct((n_slots, d_model), jnp.float32),
        compiler_params=_params(("arbitrary", "arbitrary")),
        name="moe_experts",
    )(chunk_expert, route["chunk_row"], route["chunk_subs"], route["chunk_tok"], route["chunk_real"],
      route["total_rows"], route["tok_sorted"], x, w_gu, w_gu, w_dn, b_gu4, b_gu4, b_dn4)


def _route(logits, n_exp):
    n_tok = logits.shape[0]
    top_val, top_idx = lax.top_k(logits[:, :n_exp], TOP_K)
    gate = jax.nn.softmax(top_val, axis=-1)
    n_assign = n_tok * TOP_K
    n_slots = -(-(n_assign + n_exp * (MOE_SUB - 1)) // MOE_SUB) * MOE_SUB
    flat_e = top_idx.reshape(-1)
    onehot = (flat_e[:, None] == jnp.arange(n_exp, dtype=flat_e.dtype)[None, :]).astype(jnp.int32)
    before = jnp.cumsum(onehot, axis=0) - onehot
    rank = jnp.take_along_axis(before, flat_e[:, None], axis=1)[:, 0]
    counts = onehot.sum(axis=0)
    padded = (counts + MOE_SUB - 1) // MOE_SUB * MOE_SUB
    pad_end = jnp.cumsum(padded)
    pad_start = pad_end - padded
    dest = (pad_start[flat_e] + rank).astype(jnp.int32)
    tok_sorted = (jnp.argsort(flat_e, stable=True) // TOP_K).astype(jnp.int32)
    grp_start = jnp.cumsum(counts) - counts
    rows = MOE_SUB * MOE_CHUNK_SUBS
    n_chunks = n_slots // rows + n_exp + 1
    per_exp = (padded + rows - 1) // rows
    chunk_end = jnp.cumsum(per_exp)
    cid = jnp.arange(n_chunks)
    exp_of = jnp.minimum(jnp.searchsorted(chunk_end, cid, side='right'), n_exp - 1)
    within = cid - (chunk_end - per_exp)[exp_of]
    live = cid < chunk_end[-1]
    row0 = pad_start[exp_of] + within * rows
    subs = jnp.clip(padded[exp_of] - within * rows, 0, rows) // MOE_SUB
    last_exp = exp_of[jnp.maximum(chunk_end[-1] - 1, 0)]
    as_i32 = lambda a: a.astype(jnp.int32)
    route = dict(
        n_slots=n_slots,
        chunk_expert=as_i32(jnp.where(live, exp_of, last_exp)),
        chunk_row=as_i32(jnp.where(live, row0, 0) // MOE_SUB),
        chunk_subs=as_i32(jnp.where(live, subs, 0)),
        chunk_tok=as_i32(jnp.where(live, grp_start[exp_of] + within * rows, 0)),
        chunk_real=as_i32(jnp.where(live, jnp.clip(counts[exp_of] - within * rows, 1, rows), 1)),
        total_rows=as_i32(pad_end[-1:]),
        tok_sorted=tok_sorted,
    )
    return gate, dest, route


def _combine_kernel(x_ref, y0_ref, y1_ref, y2_ref, y3_ref, gate_ref, g_ref, b_ref, o_ref, ob_ref, *, alpha):
    h = alpha * x_ref[...]
    for k, y_ref in enumerate((y0_ref, y1_ref, y2_ref, y3_ref)):
        h = h + gate_ref[:, k:k + 1] * y_ref[...]
    x2 = _layer_norm(h, g_ref[...], b_ref[...])
    o_ref[...] = x2
    ob_ref[...] = _mxu(x2)


def _combine(x1, yb_k, gate, g, b, alpha, tm):
    m, d_model = x1.shape
    assert m % tm == 0 and len(yb_k) == TOP_K
    row = lambda w: pl.BlockSpec((tm, w), lambda i: (i, 0))
    const = lambda a: pl.BlockSpec(a.shape, lambda i: (0,) * a.ndim)
    g, b = g.reshape(1, -1), b.reshape(1, -1)
    gate = jnp.pad(gate, ((0, 0), (0, LANES - TOP_K)))
    return pl.pallas_call(
        functools.partial(_combine_kernel, alpha=alpha),
        grid=(m // tm,),
        in_specs=[row(d_model)] * (1 + TOP_K) + [row(LANES), const(g), const(b)],
        out_specs=[row(d_model), row(d_model)],
        out_shape=[jax.ShapeDtypeStruct((m, d_model), jnp.float32),
                   jax.ShapeDtypeStruct((m, d_model), MXU_DTYPE)],
        compiler_params=_params(("parallel",)),
        name="combine",
    )(x1, *yb_k, gate, g, b)


def _cache_roll_kernel(*refs):
    n = (len(refs) - 1) // 3
    olds, news, outs, sem = refs[:n], refs[n:2 * n], refs[2 * n:3 * n], refs[3 * n]
    copies = []
    for old, new, out in zip(olds, news, outs):
        depth, _, n_past = old.shape[:3]
        n_new = new.shape[2]
        keep = n_past - n_new
        for layer in range(depth):
            copies.append(pltpu.make_async_copy(old.at[layer, :, pl.ds(n_new, keep)],
                                                out.at[layer, :, pl.ds(0, keep)], sem))
            copies.append(pltpu.make_async_copy(new.at[layer], out.at[layer, :, pl.ds(keep, n_new)], sem))
    for cp in copies:
        cp.start()
    for cp in copies:
        cp.wait()


def _cache_roll(caches, new_rows):
    for cch, new in zip(caches, new_rows):
        assert new.shape[2] < cch.shape[2] and new.dtype == cch.dtype
    any_spec = pl.BlockSpec(memory_space=pl.ANY)
    return pl.pallas_call(
        _cache_roll_kernel,
        in_specs=[any_spec] * (2 * len(caches)),
        out_specs=[any_spec] * len(caches),
        out_shape=[jax.ShapeDtypeStruct(c.shape, c.dtype) for c in caches],
        scratch_shapes=[pltpu.SemaphoreType.DMA(())],
        name="cache_roll",
    )(*caches, *new_rows)


def _forward(x_prompt, x_sample, cache_kv_w128, cache_kv_w512, cache_kv_w2048, state_ssm, w_in, w_attn_proj,
             ssm_lambda_re, ssm_lambda_im, ssm_log_dt, ssm_b_re, ssm_b_im, ssm_c_re, ssm_c_im, ssm_d, w_glu,
             b_glu, w_ssm_proj, w_o, ln1_g, ln1_b, router_w, router_b, w_gate_up, b_gate_up, w_down, b_down,
             ln2_g, ln2_b):
    caches = (cache_kv_w128, cache_kv_w512, cache_kv_w2048)
    bsz, seq, d_model = x_prompt.shape
    nreq, n_new, _ = x_sample.shape
    depth = w_in.shape[0]
    n_exp = router_w.shape[2]
    alpha = (2 * depth) ** 0.25
    n_p = bsz * seq
    n_s = nreq * SAMPLE_ROWS
    assert n_new <= SAMPLE_ROWS

    xp = x_prompt.reshape(n_p, d_model)
    xs = jnp.pad(x_sample, ((0, 0), (0, SAMPLE_ROWS - n_new), (0, 0))).reshape(n_s, d_model)
    xpb, xsb = _mxu(xp), _mxu(xs)
    kv_p = [[] for _ in ATTN_PATTERNS]
    kv_s = [[] for _ in ATTN_PATTERNS]
    h_p, h_s = [], []
    tm_in = 1024 if n_p % 1024 == 0 else 256

    for layer in range(depth):
        w_in_l = _mxu(w_in[layer])
        w_glu_l, w_ap_l, w_sp_l, w_o_l = (_mxu(w[layer]) for w in (w_glu, w_attn_proj, w_ssm_proj, w_o))
        rw_l = jnp.concatenate(_split_hi_lo(jnp.pad(router_w[layer], ((0, 0), (0, LANES - n_exp)))), axis=1)
        rb_l = jnp.pad(router_b[layer], (0, LANES - n_exp)).reshape(1, LANES)
        lam = (ssm_lambda_re[layer], ssm_lambda_im[layer], ssm_log_dt[layer], ssm_b_re[layer], ssm_b_im[layer],
               ssm_c_re[layer], ssm_c_im[layer])
        ssm_w = ssm_d.shape[1]
        u_lo = 3 * ATTN_W

        zp = _matmul(xpb, w_in_l, tm_in, 1024)
        z3 = zp.reshape(bsz, seq, -1)
        parts_p = [a for g in range(N_GROUPS) for a in _attn_prompt(z3, g)]
        ut = _u_proj(xpb, w_in_l[:, u_lo:u_lo + ssm_w], ssm_w // SSM_C)
        ys_p, h_last = _ssm_prompt(ut, bsz, seq, _ssm_operators(*lam, SSM_CHUNK))
        h_p.append(h_last)
        for g, (window, _) in enumerate(ATTN_PATTERNS):
            keep = min(window, seq)
            k_g = z3[:, seq - keep:, ATTN_W + g * GROUP_W:ATTN_W + (g + 1) * GROUP_W]
            v_g = z3[:, seq - keep:, 2 * ATTN_W + g * GROUP_W:2 * ATTN_W + (g + 1) * GROUP_W]
            kv_p[g].append(jnp.stack([k_g, v_g], axis=2).reshape(bsz, keep, 2, HEADS, HD))
        mg_p = _mix(parts_p, ys_p, zp, ssm_d[layer], w_glu_l, b_glu[layer], w_ap_l, w_sp_l, 128)
        x1_p, lg_p = _out_proj(mg_p, xp, w_o_l, ln1_g[layer], ln1_b[layer], rw_l, rb_l, alpha, 256)

        zs = _matmul(xsb, w_in_l, n_s, 1024)
        parts_s = _attn_sample(zs, caches, layer, nreq)
        ys_s, h_new = _ssm_sample(zs[:, u_lo:u_lo + ssm_w], state_ssm[layer], nreq, n_new,
                                  _ssm_operators(*lam, n_new))
        h_s.append(h_new)
        zs3 = zs.reshape(nreq, SAMPLE_ROWS, -1)[:, :n_new]
        for g in range(N_GROUPS):
            k_g = zs3[:, :, ATTN_W + g * GROUP_W:ATTN_W + (g + 1) * GROUP_W]
            v_g = zs3[:, :, 2 * ATTN_W + g * GROUP_W:2 * ATTN_W + (g + 1) * GROUP_W]
            kv_s[g].append(jnp.stack([k_g, v_g], axis=2).reshape(nreq, n_new, 2, HEADS, HD))
        mg_s = _mix(parts_s, ys_s, zs, ssm_d[layer], w_glu_l, b_glu[layer], w_ap_l, w_sp_l, n_s)
        x1_s, lg_s = _out_proj(mg_s, xs, w_o_l, ln1_g[layer], ln1_b[layer], rw_l, rb_l, alpha, n_s)

        logits = jnp.concatenate([lg_p, lg_s], axis=0)
        gate, dest, route = _route(logits, n_exp)
        yb = _moe_experts(jnp.concatenate([x1_p, x1_s], axis=0), route, w_gate_up, b_gate_up, w_down, b_down, layer)
        dest = dest.reshape(n_p + n_s, TOP_K)
        xp, xpb = _combine(x1_p, [yb[dest[:n_p, k]] for k in range(TOP_K)], gate[:n_p],
                           ln2_g[layer], ln2_b[layer], alpha, 256)
        xs, xsb = _combine(x1_s, [yb[dest[n_p:, k]] for k in range(TOP_K)], gate[n_p:],
                           ln2_g[layer], ln2_b[layer], alpha, n_s)

    y_prompt = xp.reshape(bsz, seq, d_model)
    y_sample = xs.reshape(nreq, SAMPLE_ROWS, d_model)[:, :n_new]
    kv_s_out = _cache_roll(caches, [jnp.stack(rows).astype(c.dtype) for rows, c in zip(kv_s, caches)])
    return (y_prompt, y_sample,
            jnp.stack(kv_p[0]), jnp.stack(kv_p[1]), jnp.stack(kv_p[2]), jnp.stack(h_p),
            kv_s_out[0], kv_s_out[1], kv_s_out[2], jnp.stack(h_s))


_forward_jit = jax.jit(_forward)


def kernel(x_prompt, x_sample, cache_kv_w128, cache_kv_w512, cache_kv_w2048, state_ssm, w_in, w_attn_proj,
           ssm_lambda_re, ssm_lambda_im, ssm_log_dt, ssm_b_re, ssm_b_im, ssm_c_re, ssm_c_im, ssm_d, w_glu, b_glu,
           w_ssm_proj, w_o, ln1_g, ln1_b, router_w, router_b, w_gate_up, b_gate_up, w_down, b_down, ln2_g, ln2_b):
    return _forward_jit(x_prompt, x_sample, cache_kv_w128, cache_kv_w512, cache_kv_w2048, state_ssm, w_in,
                        w_attn_proj, ssm_lambda_re, ssm_lambda_im, ssm_log_dt, ssm_b_re, ssm_b_im, ssm_c_re,
                        ssm_c_im, ssm_d, w_glu, b_glu, w_ssm_proj, w_o, ln1_g, ln1_b, router_w, router_b,
                        w_gate_up, b_gate_up, w_down, b_down, ln2_g, ln2_b)
```

```python
import functools
import math

import jax
import jax.numpy as jnp
from jax import lax
from jax.experimental import pallas as pl
from jax.experimental.pallas import tpu as pltpu

ATTN_PATTERNS = ((128, 1), (512, 4), (2048, 16))
N_GROUPS = len(ATTN_PATTERNS)
HEADS = 4
HD = 128
GROUP_W = HEADS * HD
ATTN_W = N_GROUPS * GROUP_W
N_HEADS = N_GROUPS * HEADS
ALIBI_SLOPES = tuple(2.0 ** (-8.0 * (h + 1) / N_HEADS) for h in range(N_HEADS))
QB = 128
ATTN_BLOCK = QB * max(d for _, d in ATTN_PATTERNS)
SSM_C = 16
SSM_N = 64
TOP_K = 4
SWIGLU_LIMIT = 7.0
SWIGLU_ALPHA = 1.702
LN_EPS = 1e-5

LANES = 128
SUBLANES = 8
VMEM_LIMIT_BYTES = 56 * 1024 * 1024

SSM_CHUNK = 16
SSM_GROUP_BLOCK = 8
SAMPLE_ROWS = 16
MOE_SUB = 256
MOE_CHUNK_SUBS = 8
MOE_FF_TILE = 256
MOE_ROW_UNROLL = 8
MXU_DTYPE = jnp.bfloat16
NEG_BIG = -1e30


def _params(sem):
    return pltpu.CompilerParams(dimension_semantics=sem, vmem_limit_bytes=VMEM_LIMIT_BYTES)


def _mxu(a):
    return a.astype(MXU_DTYPE)


def _dot(a, b):
    return jnp.dot(a, b, preferred_element_type=jnp.float32)


def _mm_kernel(x_ref, w_ref, o_ref, wb):
    @pl.when(pl.program_id(1) == 0)
    def _():
        wb[...] = _mxu(w_ref[...])

    o_ref[...] = _dot(_mxu(x_ref[...]), wb[...])


def _matmul(x, w, layer, tm, tn):
    m, k = x.shape
    n = w.shape[2]
    assert m % tm == 0 and n % tn == 0
    return pl.pallas_call(
        _mm_kernel,
        grid=(n // tn, m // tm),
        in_specs=[pl.BlockSpec((tm, k), lambda j, i: (i, 0)),
                  pl.BlockSpec((None, k, tn), lambda j, i: (layer, 0, j))],
        out_specs=pl.BlockSpec((tm, tn), lambda j, i: (i, j)),
        out_shape=jax.ShapeDtypeStruct((m, n), jnp.float32),
        scratch_shapes=[pltpu.VMEM((k, tn), MXU_DTYPE)],
        compiler_params=_params(("arbitrary", "arbitrary")),
        name="in_proj",
    )(x, w)


def _attn_prompt_kernel(q_ref, kc_ref, kp_ref, vc_ref, vp_ref, o_ref, l_ref, kf, vf, *, tile, slopes, dilation):
    h = pl.program_id(1)
    n = pl.program_id(2)
    slope = jnp.float32(slopes[HEADS - 1])
    for hh in range(HEADS - 2, -1, -1):
        slope = jnp.where(h == hh, jnp.float32(slopes[hh]), slope)
    a = lax.broadcasted_iota(jnp.int32, (QB, 2 * QB), 0)
    c = lax.broadcasted_iota(jnp.int32, (QB, 2 * QB), 1)
    rel = a + QB - c
    band = (rel >= 0) & (rel <= QB)
    first_lo = jnp.where(n > 0, 0, QB)
    bias = slope * (rel * dilation).astype(jnp.float32)
    scale = HD ** -0.5

    def rows(r, start, count):
        if dilation == 1:
            return pl.ds(start, count)
        return pl.ds(start * dilation + r, count, stride=dilation)

    for r in range(dilation):
        kf[0:QB, :] = _mxu(kp_ref[rows(r, 0, QB), :])
        kf[QB:, :] = _mxu(kc_ref[rows(r, 0, tile), :])
        vf[0:QB, :] = _mxu(vp_ref[rows(r, 0, QB), :])
        vf[QB:, :] = _mxu(vc_ref[rows(r, 0, tile), :])
        for i in range(tile // QB):
            valid = (band & (c >= first_lo)) if i == 0 else band
            q = _mxu(q_ref[rows(r, i * QB, QB), :])
            k = kf[i * QB:(i + 2) * QB, :]
            v = vf[i * QB:(i + 2) * QB, :]
            s = lax.dot_general(q, k, (((1,), (1,)), ((), ())), preferred_element_type=jnp.float32) * scale
            s = jnp.where(valid, s - bias, NEG_BIG)
            m = jnp.max(s, axis=-1, keepdims=True)
            p = jnp.exp(s - m)
            l = jnp.sum(p, axis=-1, keepdims=True)
            o_ref[rows(r, i * QB, QB), :] = _dot(_mxu(p), v) / l
            l_ref[rows(r, i * QB, QB), :] = jnp.broadcast_to(m + jnp.log(l), (QB, HD))


def _attn_prompt(z3, g):
    window, dilation = ATTN_PATTERNS[g]
    assert window // dilation == QB
    bsz, seq, zc = z3.shape
    blk = ATTN_BLOCK
    tile = blk // dilation
    assert seq % blk == 0 and zc % HD == 0
    qcol, kcol, vcol = g * HEADS, (ATTN_W // HD) + g * HEADS, 2 * (ATTN_W // HD) + g * HEADS
    sub = blk // (QB * dilation)

    def cur(col):
        return pl.BlockSpec((None, blk, HD), lambda b, h, n: (b, n, col + h))

    def prev(col):
        return pl.BlockSpec((None, QB * dilation, HD), lambda b, h, n: (b, jnp.maximum(n * sub - 1, 0), col + h))

    slopes = ALIBI_SLOPES[g * HEADS:(g + 1) * HEADS]
    out_spec = pl.BlockSpec((None, blk, HD), lambda b, h, n: (b, n, h))
    out_shape = jax.ShapeDtypeStruct((bsz, seq, GROUP_W), jnp.float32)
    o, lse = pl.pallas_call(
        functools.partial(_attn_prompt_kernel, tile=tile, slopes=slopes, dilation=dilation),
        grid=(bsz, HEADS, seq // blk),
        in_specs=[cur(qcol), cur(kcol), prev(kcol), cur(vcol), prev(vcol)],
        out_specs=[out_spec, out_spec],
        out_shape=[out_shape, out_shape],
        scratch_shapes=[pltpu.VMEM((tile + QB, HD), MXU_DTYPE),
                        pltpu.VMEM((tile + QB, HD), MXU_DTYPE)],
        compiler_params=_params(("parallel", "parallel", "arbitrary")),
        name="attn_prompt_g%d" % g,
    )(z3, z3, z3, z3, z3)
    return o.reshape(bsz * seq, GROUP_W), lse.reshape(bsz * seq, GROUP_W)


def _attn_sample_kernel(z_ref, c0_ref, c1_ref, c2_ref, o0_ref, l0_ref, o1_ref, l1_ref, o2_ref, l2_ref):
    rows = SAMPLE_ROWS
    scale = HD ** -0.5
    for g, (c_ref, o_ref, l_ref) in enumerate(((c0_ref, o0_ref, l0_ref), (c1_ref, o1_ref, l1_ref),
                                                (c2_ref, o2_ref, l2_ref))):
        window, dilation = ATTN_PATTERNS[g]
        n_past = c_ref.shape[0] // (2 * HEADS)
        s_idx = lax.broadcasted_iota(jnp.int32, (rows, n_past), 0)
        i_idx = lax.broadcasted_iota(jnp.int32, (rows, n_past), 1)
        rel_c = n_past + s_idx - i_idx
        valid_c = (rel_c <= window) & ((rel_c & (dilation - 1)) == 0)
        dist_c = rel_c.astype(jnp.float32)
        s2 = lax.broadcasted_iota(jnp.int32, (rows, rows), 0)
        t2 = lax.broadcasted_iota(jnp.int32, (rows, rows), 1)
        rel_n = s2 - t2
        valid_n = (rel_n >= 0) & ((rel_n & (dilation - 1)) == 0)
        dist_n = rel_n.astype(jnp.float32)
        for h in range(HEADS):
            slope = ALIBI_SLOPES[g * HEADS + h]
            col = g * GROUP_W + h * HD
            q = _mxu(z_ref[:, col:col + HD])
            kn = _mxu(z_ref[:, ATTN_W + col:ATTN_W + col + HD])
            vn = _mxu(z_ref[:, 2 * ATTN_W + col:2 * ATTN_W + col + HD])
            kc = _mxu(c_ref[pl.ds(h, n_past, stride=2 * HEADS), :])
            vc = _mxu(c_ref[pl.ds(HEADS + h, n_past, stride=2 * HEADS), :])
            sc = lax.dot_general(q, kc, (((1,), (1,)), ((), ())), preferred_element_type=jnp.float32) * scale
            sn = lax.dot_general(q, kn, (((1,), (1,)), ((), ())), preferred_element_type=jnp.float32) * scale
            sc = jnp.where(valid_c, sc - slope * dist_c, NEG_BIG)
            sn = jnp.where(valid_n, sn - slope * dist_n, NEG_BIG)
            m = jnp.maximum(jnp.max(sc, axis=-1, keepdims=True), jnp.max(sn, axis=-1, keepdims=True))
            pc = jnp.exp(sc - m)
            pn = jnp.exp(sn - m)
            l = jnp.sum(pc, axis=-1, keepdims=True) + jnp.sum(pn, axis=-1, keepdims=True)
            o = (_dot(_mxu(pc), vc) + _dot(_mxu(pn), vn)) / l
            o_ref[:, h * HD:(h + 1) * HD] = o
            l_ref[:, h * HD:(h + 1) * HD] = jnp.broadcast_to(m + jnp.log(l), (rows, HD))


def _attn_sample(zs, caches, layer, nreq):
    zc = zs.shape[1]
    z3 = zs.reshape(nreq, SAMPLE_ROWS, zc)
    cache_specs, cache_args = [], []
    for cch in caches:
        depth, nb, n_past = cch.shape[:3]
        assert cch.shape[3:] == (2, HEADS, HD)
        cache_args.append(cch.reshape(depth, nb, n_past * 2 * HEADS, HD))
        cache_specs.append(pl.BlockSpec((None, None, n_past * 2 * HEADS, HD), lambda b: (layer, b, 0, 0)))
    out_spec = pl.BlockSpec((None, SAMPLE_ROWS, GROUP_W), lambda b: (b, 0, 0))
    out_shape = jax.ShapeDtypeStruct((nreq, SAMPLE_ROWS, GROUP_W), jnp.float32)
    outs = pl.pallas_call(
        _attn_sample_kernel,
        grid=(nreq,),
        in_specs=[pl.BlockSpec((None, SAMPLE_ROWS, zc), lambda b: (b, 0, 0))] + cache_specs,
        out_specs=[out_spec] * (2 * N_GROUPS),
        out_shape=[out_shape] * (2 * N_GROUPS),
        compiler_params=_params(("parallel",)),
        name="attn_sample",
    )(z3, *cache_args)
    return [o.reshape(nreq * SAMPLE_ROWS, GROUP_W) for o in outs]


def _ssm_operators(lam_re, lam_im, log_dt, b_re, b_im, c_re, c_im, chunk):
    f32 = jnp.float32
    hi = lax.Precision.HIGHEST
    lam_re, lam_im = lam_re.astype(f32), lam_im.astype(f32)
    dt = jnp.exp(log_dt.astype(f32))[:, None]
    decay = jnp.exp(lam_re * dt)
    a_re = decay * jnp.cos(lam_im * dt)
    a_im = decay * jnp.sin(lam_im * dt)
    den = lam_re * lam_re + lam_im * lam_im
    f_re = ((a_re - 1.0) * lam_re + a_im * lam_im) / den
    f_im = (a_im * lam_re - (a_re - 1.0) * lam_im) / den
    b_re, b_im = b_re.astype(f32), b_im.astype(f32)
    bb_re = f_re[..., None] * b_re - f_im[..., None] * b_im
    bb_im = f_re[..., None] * b_im + f_im[..., None] * b_re
    c_re, c_im = c_re.astype(f32), c_im.astype(f32)

    def power(p):
        p = jnp.asarray(p, f32)[..., None, None]
        mag = jnp.exp(lam_re * dt * p)
        ang = lam_im * dt * p
        return mag * jnp.cos(ang), mag * jnp.sin(ang)

    n_g = lam_re.shape[0]
    ap_re, ap_im = power(jnp.arange(chunk + 1))
    abb_re = ap_re[:chunk, :, :, None] * bb_re - ap_im[:chunk, :, :, None] * bb_im
    abb_im = ap_re[:chunk, :, :, None] * bb_im + ap_im[:chunk, :, :, None] * bb_re
    kern = (jnp.einsum('gcn,tgnd->tgcd', c_re, abb_re, precision=hi)
            - jnp.einsum('gcn,tgnd->tgcd', c_im, abb_im, precision=hi))
    kern = jnp.concatenate([kern, jnp.zeros_like(kern[:1])], axis=0)
    lag = jnp.arange(chunk)[:, None] - jnp.arange(chunk)[None, :]
    m_op = kern[jnp.where(lag >= 0, lag, chunk)]
    m_op = m_op.transpose(2, 0, 3, 1, 4).reshape(n_g, chunk * SSM_C, chunk * SSM_C)
    rev = chunk - 1 - jnp.arange(chunk)
    p_re = abb_re[rev].transpose(1, 2, 0, 3).reshape(n_g, SSM_N, chunk * SSM_C)
    p_im = abb_im[rev].transpose(1, 2, 0, 3).reshape(n_g, SSM_N, chunk * SSM_C)
    p_op = jnp.concatenate([p_re, p_im], axis=1)
    a1_re, a1_im = ap_re[1:], ap_im[1:]
    q_re = c_re[None] * a1_re[:, :, None, :] - c_im[None] * a1_im[:, :, None, :]
    q_im = -(c_re[None] * a1_im[:, :, None, :] + c_im[None] * a1_re[:, :, None, :])
    q_op = jnp.concatenate([q_re, q_im], axis=-1).transpose(1, 0, 2, 3).reshape(n_g, chunk * SSM_C, 2 * SSM_N)
    return m_op, p_op, q_op, power


def _u_proj_kernel(x_ref, w_ref, u_ref, slab):
    n_slab, n_tok, _ = slab.shape
    n_chunk = n_tok // SSM_CHUNK
    groups_per_slab = LANES // SSM_C
    for j2 in range(n_slab // 2):
        zt = _dot(_mxu(x_ref[...]), _mxu(w_ref[:, 2 * j2 * LANES:2 * (j2 + 1) * LANES]))
        slab[2 * j2] = zt[:, :LANES]
        slab[2 * j2 + 1] = zt[:, LANES:]
    for j in range(n_slab):
        for s in range(SSM_CHUNK):
            piece = slab[j, pl.ds(s, n_chunk, stride=SSM_CHUNK), :].T
            for gg in range(groups_per_slab):
                u_ref[j * groups_per_slab + gg, s * SSM_C:(s + 1) * SSM_C, :] = _mxu(piece[gg * SSM_C:(gg + 1) * SSM_C, :])


def _u_proj(x, w, layer, col0, n_g):
    n_tok, k = x.shape
    blk_tok = LANES * SSM_CHUNK
    halves = 2
    cols = n_g * SSM_C // halves
    assert n_tok % blk_tok == 0 and cols % (2 * LANES) == 0 and col0 % cols == 0
    lc = SSM_CHUNK * SSM_C
    return pl.pallas_call(
        _u_proj_kernel,
        grid=(n_tok // blk_tok, halves),
        in_specs=[pl.BlockSpec((blk_tok, k), lambda i, hf: (i, 0)),
                  pl.BlockSpec((None, k, cols), lambda i, hf: (layer, 0, col0 // cols + hf))],
        out_specs=pl.BlockSpec((n_g // halves, lc, LANES), lambda i, hf: (hf, 0, i)),
        out_shape=jax.ShapeDtypeStruct((n_g, lc, n_tok // SSM_CHUNK), MXU_DTYPE),
        scratch_shapes=[pltpu.VMEM((cols // LANES, blk_tok, LANES), jnp.float32)],
        compiler_params=_params(("parallel", "arbitrary")),
        name="u_proj",
    )(x, w)


def _ssm_prompt_kernel(u_ref, m_ref, p_ref, q_ref, pw_ref, y_ref, h_ref, ybuf, *, chunks_per_seq, n_steps):
    n_cols = u_ref.shape[2]
    lane = lax.broadcasted_iota(jnp.int32, (SSM_N, n_cols), 1) % chunks_per_seq
    for gi in range(u_ref.shape[0]):
        u = u_ref[gi]
        xc = _dot(p_ref[gi], u)
        h_re, h_im = xc[:SSM_N], xc[SSM_N:]
        for j in range(n_steps):
            sh = 1 << j
            a_re = pw_ref[gi, :, j:j + 1]
            a_im = pw_ref[gi, :, n_steps + j:n_steps + j + 1]
            keep = lane >= sh
            s_re = jnp.where(keep, pltpu.roll(h_re, sh, 1), 0.0)
            s_im = jnp.where(keep, pltpu.roll(h_im, sh, 1), 0.0)
            h_re, h_im = h_re + a_re * s_re - a_im * s_im, h_im + a_re * s_im + a_im * s_re
        h_ref[gi, 0:SSM_N, :] = h_re
        h_ref[gi, SSM_N:, :] = h_im
        first = lane >= 1
        hp = jnp.concatenate([jnp.where(first, pltpu.roll(h_re, 1, 1), 0.0),
                              jnp.where(first, pltpu.roll(h_im, 1, 1), 0.0)], axis=0)
        ybuf[gi] = _dot(m_ref[gi], u) + _dot(q_ref[gi], _mxu(hp))
    for t in range(SSM_CHUNK):
        rows = jnp.concatenate([ybuf[gi, t * SSM_C:(t + 1) * SSM_C, :] for gi in range(u_ref.shape[0])], axis=0)
        for cb in range(n_cols // LANES):
            y_ref[pl.ds(cb * LANES * SSM_CHUNK + t, LANES, stride=SSM_CHUNK), :] = rows[:, cb * LANES:(cb + 1) * LANES].T


def _ssm_prompt(ut, bsz, seq, ops):
    m_op, p_op, q_op, power = ops
    n_g = m_op.shape[0]
    chunks_per_seq = seq // SSM_CHUNK
    n_cols = bsz * chunks_per_seq
    n_steps = int(math.log2(chunks_per_seq))
    assert (1 << n_steps) == chunks_per_seq and n_cols % LANES == 0
    lc = SSM_CHUNK * SSM_C
    pw_re, pw_im = power(SSM_CHUNK * (2 ** jnp.arange(n_steps)))
    pw = jnp.concatenate([pw_re, pw_im], axis=0).transpose(1, 2, 0)
    gb = LANES // SSM_C
    assert n_g % gb == 0
    y, h = pl.pallas_call(
        functools.partial(_ssm_prompt_kernel, chunks_per_seq=chunks_per_seq, n_steps=n_steps),
        grid=(n_g // gb,),
        in_specs=[pl.BlockSpec((gb, lc, n_cols), lambda i: (i, 0, 0)),
                  pl.BlockSpec((gb, lc, lc), lambda i: (i, 0, 0)),
                  pl.BlockSpec((gb, 2 * SSM_N, lc), lambda i: (i, 0, 0)),
                  pl.BlockSpec((gb, lc, 2 * SSM_N), lambda i: (i, 0, 0)),
                  pl.BlockSpec((gb, SSM_N, 2 * n_steps), lambda i: (i, 0, 0))],
        out_specs=[pl.BlockSpec((bsz * seq, LANES), lambda i: (0, i)),
                   pl.BlockSpec((gb, 2 * SSM_N, n_cols), lambda i: (i, 0, 0))],
        out_shape=[jax.ShapeDtypeStruct((bsz * seq, n_g * SSM_C), jnp.float32),
                   jax.ShapeDtypeStruct((n_g, 2 * SSM_N, n_cols), jnp.float32)],
        scratch_shapes=[pltpu.VMEM((gb, lc, n_cols), jnp.float32)],
        compiler_params=_params(("parallel",)),
        name="ssm_prompt",
    )(ut, _mxu(m_op), _mxu(p_op), _mxu(q_op), pw)
    h_last = h.reshape(n_g, 2, SSM_N, bsz, chunks_per_seq)[..., -1].transpose(3, 0, 2, 1)
    return y, h_last


def _ssm_sample_kernel(u_ref, h0_ref, m_ref, p_ref, q_ref, a_ref, y_ref, h_ref):
    for gi in range(u_ref.shape[0]):
        u = u_ref[gi]
        h0 = h0_ref[gi]
        h0_re, h0_im = h0[:SSM_N], h0[SSM_N:]
        a_re = a_ref[gi, :, 0:1]
        a_im = a_ref[gi, :, 1:2]
        xc = _dot(p_ref[gi], u)
        h_ref[gi, 0:SSM_N, :] = a_re * h0_re - a_im * h0_im + xc[:SSM_N]
        h_ref[gi, SSM_N:, :] = a_re * h0_im + a_im * h0_re + xc[SSM_N:]
        y_ref[gi] = _dot(m_ref[gi], u) + _dot(q_ref[gi], _mxu(h0))


def _ssm_sample(u, h0, nreq, n_new, ops):
    m_op, p_op, q_op, power = ops
    n_g = m_op.shape[0]
    lc = n_new * SSM_C
    ut = u.reshape(nreq, SAMPLE_ROWS, n_g, SSM_C)[:, :n_new].transpose(2, 1, 3, 0).reshape(n_g, lc, nreq)
    ut = _mxu(jnp.pad(ut, ((0, 0), (0, 0), (0, LANES - nreq))))
    h0t = h0.astype(jnp.float32).transpose(1, 3, 2, 0).reshape(n_g, 2 * SSM_N, nreq)
    h0t = jnp.pad(h0t, ((0, 0), (0, 0), (0, LANES - nreq)))
    a_re, a_im = power(n_new)
    a_op = jnp.stack([a_re, a_im], axis=-1)
    gb = SSM_GROUP_BLOCK
    y, h = pl.pallas_call(
        _ssm_sample_kernel,
        grid=(n_g // gb,),
        in_specs=[pl.BlockSpec((gb, lc, LANES), lambda i: (i, 0, 0)),
                  pl.BlockSpec((gb, 2 * SSM_N, LANES), lambda i: (i, 0, 0)),
                  pl.BlockSpec((gb, lc, lc), lambda i: (i, 0, 0)),
                  pl.BlockSpec((gb, 2 * SSM_N, lc), lambda i: (i, 0, 0)),
                  pl.BlockSpec((gb, lc, 2 * SSM_N), lambda i: (i, 0, 0)),
                  pl.BlockSpec((gb, SSM_N, 2), lambda i: (i, 0, 0))],
        out_specs=[pl.BlockSpec((gb, lc, LANES), lambda i: (i, 0, 0)),
                   pl.BlockSpec((gb, 2 * SSM_N, LANES), lambda i: (i, 0, 0))],
        out_shape=[jax.ShapeDtypeStruct((n_g, lc, LANES), jnp.float32),
                   jax.ShapeDtypeStruct((n_g, 2 * SSM_N, LANES), jnp.float32)],
        compiler_params=_params(("parallel",)),
        name="ssm_sample",
    )(ut, h0t, _mxu(m_op), _mxu(p_op), _mxu(q_op), a_op)
    y = y[:, :, :nreq].reshape(n_g, n_new, SSM_C, nreq).transpose(3, 1, 0, 2).reshape(nreq, n_new, n_g * SSM_C)
    y = jnp.pad(y, ((0, 0), (0, SAMPLE_ROWS - n_new), (0, 0))).reshape(nreq * SAMPLE_ROWS, n_g * SSM_C)
    h_new = h[:, :, :nreq].reshape(n_g, 2, SSM_N, nreq).transpose(3, 0, 2, 1)
    return y, h_new


def _mix_kernel(o0_ref, l0_ref, o1_ref, l1_ref, o2_ref, l2_ref, ys_ref, u_ref, ga_ref, gs_ref, dsk_ref, wglu_ref,
                bglu_ref, wap_ref, wsp_ref, o_ref, attn_buf):
    outs = (o0_ref, o1_ref, o2_ref)
    for j in range(HEADS):
        sl = slice(j * HD, (j + 1) * HD)
        lses = [l[:, sl] for l in (l0_ref, l1_ref, l2_ref)]
        top = jnp.maximum(jnp.maximum(lses[0], lses[1]), lses[2])
        ws = [jnp.exp(l - top) for l in lses]
        num = ws[0] * outs[0][:, sl] + ws[1] * outs[1][:, sl] + ws[2] * outs[2][:, sl]
        attn_buf[:, sl] = _mxu(num / (ws[0] + ws[1] + ws[2]))
    y = jax.nn.gelu(ys_ref[...] + dsk_ref[...] * u_ref[...])
    ssm = y * jax.nn.sigmoid(_dot(_mxu(y), wglu_ref[...]) + bglu_ref[...])
    merged = (jax.nn.sigmoid(ga_ref[...]) * _dot(attn_buf[...], wap_ref[...])
              + jax.nn.sigmoid(gs_ref[...]) * _dot(_mxu(ssm), wsp_ref[...]))
    o_ref[...] = _mxu(merged)


def _mix(parts, ys, z, d_skip, w_glu, b_glu, w_ap, w_sp, tm):
    m = ys.shape[0]
    ssm_w = ys.shape[1]
    d_model = w_ap.shape[1]
    u_off, ga_off = 3 * ATTN_W, 3 * ATTN_W + ssm_w
    gs_off = ga_off + d_model
    assert u_off % ssm_w == 0 and ga_off % d_model == 0 and m % tm == 0
    row = lambda w: pl.BlockSpec((tm, w), lambda i: (i, 0))
    const = lambda a: pl.BlockSpec(a.shape, lambda i: (0,) * a.ndim)
    d_skip, b_glu = d_skip.reshape(1, -1), b_glu.reshape(1, -1)
    return pl.pallas_call(
        _mix_kernel,
        grid=(m // tm,),
        in_specs=[row(GROUP_W)] * (2 * N_GROUPS) + [row(ssm_w),
                  pl.BlockSpec((tm, ssm_w), lambda i: (i, u_off // ssm_w)),
                  pl.BlockSpec((tm, d_model), lambda i: (i, ga_off // d_model)),
                  pl.BlockSpec((tm, d_model), lambda i: (i, gs_off // d_model)),
                  const(d_skip), const(w_glu), const(b_glu), const(w_ap), const(w_sp)],
        out_specs=row(d_model),
        out_shape=jax.ShapeDtypeStruct((m, d_model), MXU_DTYPE),
        scratch_shapes=[pltpu.VMEM((tm, GROUP_W), MXU_DTYPE)],
        compiler_params=_params(("parallel",)),
        name="mix",
    )(*parts, ys, z, z, z, d_skip, w_glu, b_glu, w_ap, w_sp)


def _layer_norm(h, g, b):
    mu = jnp.mean(h, axis=-1, keepdims=True)
    d = h - mu
    var = jnp.mean(d * d, axis=-1, keepdims=True)
    return d * lax.rsqrt(var + LN_EPS) * g + b


def _split_hi_lo(a):
    hi = _mxu(a)
    return hi, _mxu(a - hi.astype(jnp.float32))


def _out_kernel(m_ref, x_ref, wo_ref, g_ref, b_ref, rw_ref, rb_ref, x1_ref, lg_ref, *, alpha):
    h = alpha * x_ref[...] + _dot(m_ref[...], wo_ref[...])
    x1 = _layer_norm(h, g_ref[...], b_ref[...])
    x1_ref[...] = x1
    hi, lo = _split_hi_lo(x1)
    both = _dot(hi, rw_ref[...])
    lg_ref[...] = both[:, :LANES] + both[:, LANES:] + _dot(lo, rw_ref[:, :LANES]) + rb_ref[...]


def _out_proj(merged, x, w_o, g, b, rw, rb, alpha, tm):
    m, d_model = x.shape
    assert m % tm == 0
    row = lambda w: pl.BlockSpec((tm, w), lambda i: (i, 0))
    const = lambda a: pl.BlockSpec(a.shape, lambda i: (0,) * a.ndim)
    g, b = g.reshape(1, -1), b.reshape(1, -1)
    return pl.pallas_call(
        functools.partial(_out_kernel, alpha=alpha),
        grid=(m // tm,),
        in_specs=[row(d_model), row(d_model), const(w_o), const(g), const(b), const(rw), const(rb)],
        out_specs=[row(d_model), row(LANES)],
        out_shape=[jax.ShapeDtypeStruct((m, d_model), jnp.float32),
                   jax.ShapeDtypeStruct((m, LANES), jnp.float32)],
        compiler_params=_params(("parallel",)),
        name="out_proj",
    )(merged, x, w_o, g, b, rw, rb)


def _moe_kernel(ce_ref, cr_ref, cn_ref, ct_ref, cm_ref, tot_ref, tok_ref, x_hbm, wg_ref, wl_ref, wd_ref, bg_ref,
                bl_ref, bd_ref, y_hbm, xbuf, acc, wgb, wlb, wdb, sem_in, sem_out, *, n_ff_tiles, n_sub_total):
    c = pl.program_id(0)
    j = pl.program_id(1)
    nsb = cn_ref[c]
    row0 = cr_ref[c]
    tok0 = ct_ref[c]
    last_real = cm_ref[c] - 1

    def sub_rows(s):
        return pl.ds(pl.multiple_of(s * MOE_SUB, MOE_SUB), MOE_SUB)

    def x_copy(i):
        tok = tok_ref[tok0 + jnp.minimum(i, last_real)]
        return pltpu.make_async_copy(x_hbm.at[pl.ds(tok, 1)], xbuf.at[pl.ds(i, 1)], sem_in)

    def y_copy(s, first_sub):
        return pltpu.make_async_copy(acc.at[sub_rows(s)], y_hbm.at[sub_rows(first_sub + s)], sem_out)

    def for_subs(n, fn):
        def body(s, carry):
            fn(s)
            return carry
        lax.fori_loop(0, n, body, 0)

    def for_rows(n_sub, fn):
        def body(g, carry):
            for k in range(MOE_ROW_UNROLL):
                fn(g * MOE_ROW_UNROLL + k)
            return carry
        lax.fori_loop(0, n_sub * (MOE_SUB // MOE_ROW_UNROLL), body, 0)

    @pl.when((j == 0) & (nsb > 0))
    def _():
        for_rows(nsb, lambda i: x_copy(i).start())

    @pl.when((j == 0) & (c > 0))
    def _():
        prev = jnp.maximum(c - 1, 0)
        for_subs(cn_ref[prev], lambda s: y_copy(s, cr_ref[prev]).wait())

    @pl.when((j == 0) & (nsb > 0))
    def _():
        def init(s):
            acc[sub_rows(s), :] = jnp.broadcast_to(bd_ref[...], (MOE_SUB, acc.shape[1]))

        for_subs(nsb, init)
        for_rows(nsb, lambda i: x_copy(i).wait())

    @pl.when(nsb > 0)
    def _():
        wgb[...] = _mxu(wg_ref[...])
        wlb[...] = _mxu(wl_ref[...])
        wdb[...] = _mxu(wd_ref[...])

        def sub(s):
            rows = sub_rows(s)
            xs = _mxu(xbuf[rows, :])
            glu = jnp.minimum(_dot(xs, wgb[...]) + bg_ref[...], SWIGLU_LIMIT)
            lin = jnp.clip(_dot(xs, wlb[...]) + bl_ref[...], -SWIGLU_LIMIT, SWIGLU_LIMIT)
            act = glu * jax.nn.sigmoid(SWIGLU_ALPHA * glu) * (lin + 1.0)
            acc[rows, :] += _dot(_mxu(act), wdb[...])

        def pair(p):
            sub(2 * p)
            sub(2 * p + 1)

        for_subs(nsb // 2, pair)

        @pl.when(nsb % 2 == 1)
        def _():
            sub(nsb - 1)

    @pl.when((j == n_ff_tiles - 1) & (nsb > 0))
    def _():
        for_subs(nsb, lambda s: y_copy(s, row0).start())

    @pl.when((c == pl.num_programs(0) - 1) & (j == n_ff_tiles - 1))
    def _():
        for_subs(nsb, lambda s: y_copy(s, row0).wait())
        first = tot_ref[0] // MOE_SUB
        acc[0:MOE_SUB, :] = jnp.zeros((MOE_SUB, acc.shape[1]), acc.dtype)

        def z_copy(s):
            return pltpu.make_async_copy(acc.at[pl.ds(0, MOE_SUB)], y_hbm.at[sub_rows(first + s)], sem_out)

        for_subs(n_sub_total - first, lambda s: z_copy(s).start())
        for_subs(n_sub_total - first, lambda s: z_copy(s).wait())


def _moe_experts(x, route, w_gu, b_gu, w_dn, b_dn, layer):
    chunk_expert = route["chunk_expert"]
    n_slots = route["n_slots"]
    d_model = x.shape[1]
    d_ff = w_dn.shape[2]
    tf = MOE_FF_TILE
    n_ff_tiles = d_ff // tf
    n_chunks = chunk_expert.shape[0]
    rows = MOE_SUB * MOE_CHUNK_SUBS
    n_exp = w_gu.shape[1]
    b_gu4 = b_gu.reshape(b_gu.shape[0], n_exp, 1, 2 * d_ff)
    b_dn4 = b_dn.reshape(b_dn.shape[0], n_exp, 1, d_model)

    def ff(j, c, cn):
        return jnp.where(cn[c] > 0, j, n_ff_tiles - 1)

    def wspec(shape, index):
        return pl.BlockSpec((None, None) + shape,
                            lambda c, j, ce, cr, cn, *_: (layer, ce[c]) + index(ff(j, c, cn)))

    grid_spec = pltpu.PrefetchScalarGridSpec(
        num_scalar_prefetch=7,
        grid=(n_chunks, n_ff_tiles),
        in_specs=[
            pl.BlockSpec(memory_space=pl.ANY),
            wspec((d_model, tf), lambda t: (0, t)),
            wspec((d_model, tf), lambda t: (0, n_ff_tiles + t)),
            wspec((tf, d_model), lambda t: (t, 0)),
            wspec((1, tf), lambda t: (0, t)),
            wspec((1, tf), lambda t: (0, n_ff_tiles + t)),
            wspec((1, d_model), lambda t: (0, 0)),
        ],
        out_specs=pl.BlockSpec(memory_space=pl.ANY),
        scratch_shapes=[
            pltpu.VMEM((rows, d_model), jnp.float32),
            pltpu.VMEM((rows, d_model), jnp.float32),
            pltpu.VMEM((d_model, tf), MXU_DTYPE),
            pltpu.VMEM((d_model, tf), MXU_DTYPE),
            pltpu.VMEM((tf, d_model), MXU_DTYPE),
            pltpu.SemaphoreType.DMA(()),
            pltpu.SemaphoreType.DMA(()),
        ],
    )
    return pl.pallas_call(
        functools.partial(_moe_kernel, n_ff_tiles=n_ff_tiles, n_sub_total=n_slots // MOE_SUB),
        grid_spec=grid_spec,
        out_shape=jax.ShapeDtypeStruct((n_slots, d_model), jnp.float32),
        compiler_params=_params(("arbitrary", "arbitrary")),
        name="moe_experts",
    )(chunk_expert, route["chunk_row"], route["chunk_subs"], route["chunk_tok"], route["chunk_real"],
      route["total_rows"], route["tok_sorted"], x, w_gu, w_gu, w_dn, b_gu4, b_gu4, b_dn4)


def _route(logits, n_exp):
    n_tok = logits.shape[0]
    top_val, top_idx = lax.top_k(logits[:, :n_exp], TOP_K)
    gate = jax.nn.softmax(top_val, axis=-1)
    n_assign = n_tok * TOP_K
    n_slots = -(-(n_assign + n_exp * (MOE_SUB - 1)) // MOE_SUB) * MOE_SUB
    flat_e = top_idx.reshape(-1)
    onehot = (flat_e[:, None] == jnp.arange(n_exp, dtype=flat_e.dtype)[None, :]).astype(jnp.int32)
    before = jnp.cumsum(onehot, axis=0) - onehot
    rank = jnp.take_along_axis(before, flat_e[:, None], axis=1)[:, 0]
    counts = onehot.sum(axis=0)
    padded = (counts + MOE_SUB - 1) // MOE_SUB * MOE_SUB
    pad_end = jnp.cumsum(padded)
    pad_start = pad_end - padded
    dest = (pad_start[flat_e] + rank).astype(jnp.int32)
    tok_sorted = (jnp.argsort(flat_e, stable=True) // TOP_K).astype(jnp.int32)
    grp_start = jnp.cumsum(counts) - counts
    rows = MOE_SUB * MOE_CHUNK_SUBS
    n_chunks = n_slots // rows + n_exp + 1
    per_exp = (padded + rows - 1) // rows
    chunk_end = jnp.cumsum(per_exp)
    cid = jnp.arange(n_chunks)
    exp_of = jnp.minimum(jnp.searchsorted(chunk_end, cid, side='right'), n_exp - 1)
    within = cid - (chunk_end - per_exp)[exp_of]
    live = cid < chunk_end[-1]
    row0 = pad_start[exp_of] + within * rows
    subs = jnp.clip(padded[exp_of] - within * rows, 0, rows) // MOE_SUB
    last_exp = exp_of[jnp.maximum(chunk_end[-1] - 1, 0)]
    as_i32 = lambda a: a.astype(jnp.int32)
    route = dict(
        n_slots=n_slots,
        chunk_expert=as_i32(jnp.where(live, exp_of, last_exp)),
        chunk_row=as_i32(jnp.where(live, row0, 0) // MOE_SUB),
        chunk_subs=as_i32(jnp.where(live, subs, 0)),
        chunk_tok=as_i32(jnp.where(live, grp_start[exp_of] + within * rows, 0)),
        chunk_real=as_i32(jnp.where(live, jnp.clip(counts[exp_of] - within * rows, 1, rows), 1)),
        total_rows=as_i32(pad_end[-1:]),
        tok_sorted=tok_sorted,
    )
    return gate, dest, route


def _combine_kernel(x_ref, y0_ref, y1_ref, y2_ref, y3_ref, gate_ref, g_ref, b_ref, o_ref, ob_ref, *, alpha):
    h = alpha * x_ref[...]
    for k, y_ref in enumerate((y0_ref, y1_ref, y2_ref, y3_ref)):
        h = h + gate_ref[:, k:k + 1] * y_ref[...]
    x2 = _layer_norm(h, g_ref[...], b_ref[...])
    o_ref[...] = x2
    ob_ref[...] = _mxu(x2)


def _combine(x1, yb_k, gate, g, b, alpha, tm):
    m, d_model = x1.shape
    assert m % tm == 0 and len(yb_k) == TOP_K
    row = lambda w: pl.BlockSpec((tm, w), lambda i: (i, 0))
    const = lambda a: pl.BlockSpec(a.shape, lambda i: (0,) * a.ndim)
    g, b = g.reshape(1, -1), b.reshape(1, -1)
    gate = jnp.pad(gate, ((0, 0), (0, LANES - TOP_K)))
    return pl.pallas_call(
        functools.partial(_combine_kernel, alpha=alpha),
        grid=(m // tm,),
        in_specs=[row(d_model)] * (1 + TOP_K) + [row(LANES), const(g), const(b)],
        out_specs=[row(d_model), row(d_model)],
        out_shape=[jax.ShapeDtypeStruct((m, d_model), jnp.float32),
                   jax.ShapeDtypeStruct((m, d_model), MXU_DTYPE)],
        compiler_params=_params(("parallel",)),
        name="combine",
    )(x1, *yb_k, gate, g, b)


def _cache_roll_kernel(*refs):
    n = (len(refs) - 1) // 3
    olds, news, outs, sem = refs[:n], refs[n:2 * n], refs[2 * n:3 * n], refs[3 * n]
    copies = []
    for old, new, out in zip(olds, news, outs):
        depth, n_req, n_past = old.shape[:3]
        n_new = new.shape[2]
        keep = n_past - n_new
        for layer in range(depth):
            for b in range(n_req):
                copies.append(pltpu.make_async_copy(old.at[layer, b, pl.ds(n_new, keep)],
                                                    out.at[layer, b, pl.ds(0, keep)], sem))
                copies.append(pltpu.make_async_copy(new.at[layer, b], out.at[layer, b, pl.ds(keep, n_new)], sem))
    for cp in copies:
        cp.start()
    for cp in copies:
        cp.wait()


def _cache_roll(caches, new_rows):
    for cch, new in zip(caches, new_rows):
        assert new.shape[2] < cch.shape[2] and new.dtype == cch.dtype and cch.shape[-1] == LANES

    def flat(a):
        return a.reshape(a.shape[0], a.shape[1], a.shape[2] * a.shape[3] * a.shape[4], LANES)

    any_spec = pl.BlockSpec(memory_space=pl.ANY)
    outs = pl.pallas_call(
        _cache_roll_kernel,
        in_specs=[any_spec] * (2 * len(caches)),
        out_specs=[any_spec] * len(caches),
        out_shape=[jax.ShapeDtypeStruct(flat(c).shape, c.dtype) for c in caches],
        scratch_shapes=[pltpu.SemaphoreType.DMA(())],
        name="cache_roll",
    )(*[flat(c) for c in caches], *[flat(n) for n in new_rows])
    return [o.reshape(c.shape) for o, c in zip(outs, caches)]


def _forward(x_prompt, x_sample, cache_kv_w128, cache_kv_w512, cache_kv_w2048, state_ssm, w_in, w_attn_proj,
             ssm_lambda_re, ssm_lambda_im, ssm_log_dt, ssm_b_re, ssm_b_im, ssm_c_re, ssm_c_im, ssm_d, w_glu,
             b_glu, w_ssm_proj, w_o, ln1_g, ln1_b, router_w, router_b, w_gate_up, b_gate_up, w_down, b_down,
             ln2_g, ln2_b):
    caches = (cache_kv_w128, cache_kv_w512, cache_kv_w2048)
    bsz, seq, d_model = x_prompt.shape
    nreq, n_new, _ = x_sample.shape
    depth = w_in.shape[0]
    n_exp = router_w.shape[2]
    alpha = (2 * depth) ** 0.25
    n_p = bsz * seq
    n_s = nreq * SAMPLE_ROWS
    assert n_new <= SAMPLE_ROWS

    xp = x_prompt.reshape(n_p, d_model)
    xs = jnp.pad(x_sample, ((0, 0), (0, SAMPLE_ROWS - n_new), (0, 0))).reshape(n_s, d_model)
    xpb, xsb = _mxu(xp), _mxu(xs)
    kv_p = [[] for _ in ATTN_PATTERNS]
    kv_s = [[] for _ in ATTN_PATTERNS]
    h_p, h_s = [], []
    tm_in = 1024 if n_p % 1024 == 0 else 256

    for layer in range(depth):
        w_glu_l, w_ap_l, w_sp_l, w_o_l = (_mxu(w[layer]) for w in (w_glu, w_attn_proj, w_ssm_proj, w_o))
        rw_l = jnp.concatenate(_split_hi_lo(jnp.pad(router_w[layer], ((0, 0), (0, LANES - n_exp)))), axis=1)
        rb_l = jnp.pad(router_b[layer], (0, LANES - n_exp)).reshape(1, LANES)
        lam = (ssm_lambda_re[layer], ssm_lambda_im[layer], ssm_log_dt[layer], ssm_b_re[layer], ssm_b_im[layer],
               ssm_c_re[layer], ssm_c_im[layer])
        ssm_w = ssm_d.shape[1]
        u_lo = 3 * ATTN_W

        zp = _matmul(xpb, w_in, layer, tm_in, 1024)
        z3 = zp.reshape(bsz, seq, -1)
        parts_p = [a for g in range(N_GROUPS) for a in _attn_prompt(z3, g)]
        ut = _u_proj(xpb, w_in, layer, u_lo, ssm_w // SSM_C)
        ys_p, h_last = _ssm_prompt(ut, bsz, seq, _ssm_operators(*lam, SSM_CHUNK))
        h_p.append(h_last)
        for g, (window, _) in enumerate(ATTN_PATTERNS):
            keep = min(window, seq)
            k_g = z3[:, seq - keep:, ATTN_W + g * GROUP_W:ATTN_W + (g + 1) * GROUP_W]
            v_g = z3[:, seq - keep:, 2 * ATTN_W + g * GROUP_W:2 * ATTN_W + (g + 1) * GROUP_W]
            kv_p[g].append(jnp.stack([k_g, v_g], axis=2).reshape(bsz, keep, 2, HEADS, HD))
        mg_p = _mix(parts_p, ys_p, zp, ssm_d[layer], w_glu_l, b_glu[layer], w_ap_l, w_sp_l, 128)
        x1_p, lg_p = _out_proj(mg_p, xp, w_o_l, ln1_g[layer], ln1_b[layer], rw_l, rb_l, alpha, 256)

        zs = _matmul(xsb, w_in, layer, n_s, 1024)
        parts_s = _attn_sample(zs, caches, layer, nreq)
        ys_s, h_new = _ssm_sample(zs[:, u_lo:u_lo + ssm_w], state_ssm[layer], nreq, n_new,
                                  _ssm_operators(*lam, n_new))
        h_s.append(h_new)
        zs3 = zs.reshape(nreq, SAMPLE_ROWS, -1)[:, :n_new]
        for g in range(N_GROUPS):
            k_g = zs3[:, :, ATTN_W + g * GROUP_W:ATTN_W + (g + 1) * GROUP_W]
            v_g = zs3[:, :, 2 * ATTN_W + g * GROUP_W:2 * ATTN_W + (g + 1) * GROUP_W]
            kv_s[g].append(jnp.stack([k_g, v_g], axis=2).reshape(nreq, n_new, 2, HEADS, HD))
        mg_s = _mix(parts_s, ys_s, zs, ssm_d[layer], w_glu_l, b_glu[layer], w_ap_l, w_sp_l, n_s)
        x1_s, lg_s = _out_proj(mg_s, xs, w_o_l, ln1_g[layer], ln1_b[layer], rw_l, rb_l, alpha, n_s)

        logits = jnp.concatenate([lg_p, lg_s], axis=0)
        gate, dest, route = _route(logits, n_exp)
        yb = _moe_experts(jnp.concatenate([x1_p, x1_s], axis=0), route, w_gate_up, b_gate_up, w_down, b_down, layer)
        dest = dest.reshape(n_p + n_s, TOP_K)
        xp, xpb = _combine(x1_p, [yb[dest[:n_p, k]] for k in range(TOP_K)], gate[:n_p],
                           ln2_g[layer], ln2_b[layer], alpha, 256)
        xs, xsb = _combine(x1_s, [yb[dest[n_p:, k]] for k in range(TOP_K)], gate[n_p:],
                           ln2_g[layer], ln2_b[layer], alpha, n_s)

    y_prompt = xp.reshape(bsz, seq, d_model)
    y_sample = xs.reshape(nreq, SAMPLE_ROWS, d_model)[:, :n_new]
    kv_s_out = _cache_roll(caches, [jnp.stack(rows).astype(c.dtype) for rows, c in zip(kv_s, caches)])
    return (y_prompt, y_sample,
            jnp.stack(kv_p[0]), jnp.stack(kv_p[1]), jnp.stack(kv_p[2]), jnp.stack(h_p),
            kv_s_out[0], kv_s_out[1], kv_s_out[2], jnp.stack(h_s))


_forward_jit = jax.jit(_forward)


def kernel(x_prompt, x_sample, cache_kv_w128, cache_kv_w512, cache_kv_w2048, state_ssm, w_in, w_attn_proj,
           ssm_lambda_re, ssm_lambda_im, ssm_log_dt, ssm_b_re, ssm_b_im, ssm_c_re, ssm_c_im, ssm_d, w_glu, b_glu,
           w_ssm_proj, w_o, ln1_g, ln1_b, router_w, router_b, w_gate_up, b_gate_up, w_down, b_down, ln2_g, ln2_b):
    return _forward_jit(x_prompt, x_sample, cache_kv_w128, cache_kv_w512, cache_kv_w2048, state_ssm, w_in,
                        w_attn_proj, ssm_lambda_re, ssm_lambda_im, ssm_log_dt, ssm_b_re, ssm_b_im, ssm_c_re,
                        ssm_c_im, ssm_d, w_glu, b_glu, w_ssm_proj, w_o, ln1_g, ln1_b, router_w, router_b,
                        w_gate_up, b_gate_up, w_down, b_down, ln2_g, ln2_b)
```

```python
import functools
import math

import jax
import jax.numpy as jnp
from jax import lax
from jax.experimental import pallas as pl
from jax.experimental.pallas import tpu as pltpu

ATTN_PATTERNS = ((128, 1), (512, 4), (2048, 16))
N_GROUPS = len(ATTN_PATTERNS)
HEADS = 4
HD = 128
GROUP_W = HEADS * HD
ATTN_W = N_GROUPS * GROUP_W
N_HEADS = N_GROUPS * HEADS
ALIBI_SLOPES = tuple(2.0 ** (-8.0 * (h + 1) / N_HEADS) for h in range(N_HEADS))
QB = 128
ATTN_BLOCK = QB * max(d for _, d in ATTN_PATTERNS)
SSM_C = 16
SSM_N = 64
TOP_K = 4
SWIGLU_LIMIT = 7.0
SWIGLU_ALPHA = 1.702
LN_EPS = 1e-5

LANES = 128
SUBLANES = 8
VMEM_LIMIT_BYTES = 56 * 1024 * 1024

SSM_CHUNK = 16
SSM_GROUP_BLOCK = 8
SAMPLE_ROWS = 16
MOE_SUB = 256
MOE_CHUNK_SUBS = 8
MOE_FF_TILE = 256
MOE_ROW_UNROLL = 8
MOE_GROUP_WIDTHS = (4, 2, 1)
MXU_DTYPE = jnp.bfloat16
NEG_BIG = -1e30


def _params(sem):
    return pltpu.CompilerParams(dimension_semantics=sem, vmem_limit_bytes=VMEM_LIMIT_BYTES)


def _mxu(a):
    return a.astype(MXU_DTYPE)


def _dot(a, b):
    return jnp.dot(a, b, preferred_element_type=jnp.float32)


def _mm_kernel(x_ref, w_ref, o_ref, wb):
    @pl.when(pl.program_id(1) == 0)
    def _():
        wb[...] = _mxu(w_ref[...])

    o_ref[...] = _dot(_mxu(x_ref[...]), wb[...])


def _matmul(x, w, layer, tm, tn):
    m, k = x.shape
    n = w.shape[2]
    assert m % tm == 0 and n % tn == 0
    return pl.pallas_call(
        _mm_kernel,
        grid=(n // tn, m // tm),
        in_specs=[pl.BlockSpec((tm, k), lambda j, i: (i, 0)),
                  pl.BlockSpec((None, k, tn), lambda j, i: (layer, 0, j))],
        out_specs=pl.BlockSpec((tm, tn), lambda j, i: (i, j)),
        out_shape=jax.ShapeDtypeStruct((m, n), jnp.float32),
        scratch_shapes=[pltpu.VMEM((k, tn), MXU_DTYPE)],
        compiler_params=_params(("arbitrary", "arbitrary")),
        name="in_proj",
    )(x, w)


def _attn_prompt_kernel(q_ref, kc_ref, kp_ref, vc_ref, vp_ref, o_ref, l_ref, kf, vf, *, tile, slopes, dilation):
    h = pl.program_id(1)
    n = pl.program_id(2)
    slope = jnp.float32(slopes[HEADS - 1])
    for hh in range(HEADS - 2, -1, -1):
        slope = jnp.where(h == hh, jnp.float32(slopes[hh]), slope)
    a = lax.broadcasted_iota(jnp.int32, (QB, 2 * QB), 0)
    c = lax.broadcasted_iota(jnp.int32, (QB, 2 * QB), 1)
    rel = a + QB - c
    band = (rel >= 0) & (rel <= QB)
    first_lo = jnp.where(n > 0, 0, QB)
    bias = slope * (rel * dilation).astype(jnp.float32)
    scale = HD ** -0.5

    def rows(r, start, count):
        if dilation == 1:
            return pl.ds(start, count)
        return pl.ds(start * dilation + r, count, stride=dilation)

    for r in range(dilation):
        kf[0:QB, :] = _mxu(kp_ref[rows(r, 0, QB), :])
        kf[QB:, :] = _mxu(kc_ref[rows(r, 0, tile), :])
        vf[0:QB, :] = _mxu(vp_ref[rows(r, 0, QB), :])
        vf[QB:, :] = _mxu(vc_ref[rows(r, 0, tile), :])
        for i in range(tile // QB):
            valid = (band & (c >= first_lo)) if i == 0 else band
            q = _mxu(q_ref[rows(r, i * QB, QB), :])
            k = kf[i * QB:(i + 2) * QB, :]
            v = vf[i * QB:(i + 2) * QB, :]
            s = lax.dot_general(q, k, (((1,), (1,)), ((), ())), preferred_element_type=jnp.float32) * scale
            s = jnp.where(valid, s - bias, NEG_BIG)
            m = jnp.max(s, axis=-1, keepdims=True)
            p = jnp.exp(s - m)
            l = jnp.sum(p, axis=-1, keepdims=True)
            o_ref[rows(r, i * QB, QB), :] = _dot(_mxu(p), v) / l
            l_ref[rows(r, i * QB, QB), :] = jnp.broadcast_to(m + jnp.log(l), (QB, HD))


def _attn_prompt(z3, g):
    window, dilation = ATTN_PATTERNS[g]
    assert window // dilation == QB
    bsz, seq, zc = z3.shape
    blk = ATTN_BLOCK
    tile = blk // dilation
    assert seq % blk == 0 and zc % HD == 0
    qcol, kcol, vcol = g * HEADS, (ATTN_W // HD) + g * HEADS, 2 * (ATTN_W // HD) + g * HEADS
    sub = blk // (QB * dilation)

    def cur(col):
        return pl.BlockSpec((None, blk, HD), lambda b, h, n: (b, n, col + h))

    def prev(col):
        return pl.BlockSpec((None, QB * dilation, HD), lambda b, h, n: (b, jnp.maximum(n * sub - 1, 0), col + h))

    slopes = ALIBI_SLOPES[g * HEADS:(g + 1) * HEADS]
    out_spec = pl.BlockSpec((None, blk, HD), lambda b, h, n: (b, n, h))
    out_shape = jax.ShapeDtypeStruct((bsz, seq, GROUP_W), jnp.float32)
    o, lse = pl.pallas_call(
        functools.partial(_attn_prompt_kernel, tile=tile, slopes=slopes, dilation=dilation),
        grid=(bsz, HEADS, seq // blk),
        in_specs=[cur(qcol), cur(kcol), prev(kcol), cur(vcol), prev(vcol)],
        out_specs=[out_spec, out_spec],
        out_shape=[out_shape, out_shape],
        scratch_shapes=[pltpu.VMEM((tile + QB, HD), MXU_DTYPE),
                        pltpu.VMEM((tile + QB, HD), MXU_DTYPE)],
        compiler_params=_params(("parallel", "parallel", "arbitrary")),
        name="attn_prompt_g%d" % g,
    )(z3, z3, z3, z3, z3)
    return o.reshape(bsz * seq, GROUP_W), lse.reshape(bsz * seq, GROUP_W)


def _attn_sample_kernel(z_ref, c0_ref, c1_ref, c2_ref, o0_ref, l0_ref, o1_ref, l1_ref, o2_ref, l2_ref):
    rows = SAMPLE_ROWS
    scale = HD ** -0.5
    for g, (c_ref, o_ref, l_ref) in enumerate(((c0_ref, o0_ref, l0_ref), (c1_ref, o1_ref, l1_ref),
                                                (c2_ref, o2_ref, l2_ref))):
        window, dilation = ATTN_PATTERNS[g]
        n_past = c_ref.shape[0] // (2 * HEADS)
        s_idx = lax.broadcasted_iota(jnp.int32, (rows, n_past), 0)
        i_idx = lax.broadcasted_iota(jnp.int32, (rows, n_past), 1)
        rel_c = n_past + s_idx - i_idx
        valid_c = (rel_c <= window) & ((rel_c & (dilation - 1)) == 0)
        dist_c = rel_c.astype(jnp.float32)
        s2 = lax.broadcasted_iota(jnp.int32, (rows, rows), 0)
        t2 = lax.broadcasted_iota(jnp.int32, (rows, rows), 1)
        rel_n = s2 - t2
        valid_n = (rel_n >= 0) & ((rel_n & (dilation - 1)) == 0)
        dist_n = rel_n.astype(jnp.float32)
        for h in range(HEADS):
            slope = ALIBI_SLOPES[g * HEADS + h]
            col = g * GROUP_W + h * HD
            q = _mxu(z_ref[:, col:col + HD])
            kn = _mxu(z_ref[:, ATTN_W + col:ATTN_W + col + HD])
            vn = _mxu(z_ref[:, 2 * ATTN_W + col:2 * ATTN_W + col + HD])
            kc = _mxu(c_ref[pl.ds(h, n_past, stride=2 * HEADS), :])
            vc = _mxu(c_ref[pl.ds(HEADS + h, n_past, stride=2 * HEADS), :])
            sc = lax.dot_general(q, kc, (((1,), (1,)), ((), ())), preferred_element_type=jnp.float32) * scale
            sn = lax.dot_general(q, kn, (((1,), (1,)), ((), ())), preferred_element_type=jnp.float32) * scale
            sc = jnp.where(valid_c, sc - slope * dist_c, NEG_BIG)
            sn = jnp.where(valid_n, sn - slope * dist_n, NEG_BIG)
            m = jnp.maximum(jnp.max(sc, axis=-1, keepdims=True), jnp.max(sn, axis=-1, keepdims=True))
            pc = jnp.exp(sc - m)
            pn = jnp.exp(sn - m)
            l = jnp.sum(pc, axis=-1, keepdims=True) + jnp.sum(pn, axis=-1, keepdims=True)
            o = (_dot(_mxu(pc), vc) + _dot(_mxu(pn), vn)) / l
            o_ref[:, h * HD:(h + 1) * HD] = o
            l_ref[:, h * HD:(h + 1) * HD] = jnp.broadcast_to(m + jnp.log(l), (rows, HD))


def _attn_sample(zs, caches, layer, nreq):
    zc = zs.shape[1]
    z3 = zs.reshape(nreq, SAMPLE_ROWS, zc)
    cache_specs, cache_args = [], []
    for cch in caches:
        depth, nb, n_past = cch.shape[:3]
        assert cch.shape[3:] == (2, HEADS, HD)
        cache_args.append(cch.reshape(depth, nb, n_past * 2 * HEADS, HD))
        cache_specs.append(pl.BlockSpec((None, None, n_past * 2 * HEADS, HD), lambda b: (layer, b, 0, 0)))
    out_spec = pl.BlockSpec((None, SAMPLE_ROWS, GROUP_W), lambda b: (b, 0, 0))
    out_shape = jax.ShapeDtypeStruct((nreq, SAMPLE_ROWS, GROUP_W), jnp.float32)
    outs = pl.pallas_call(
        _attn_sample_kernel,
        grid=(nreq,),
        in_specs=[pl.BlockSpec((None, SAMPLE_ROWS, zc), lambda b: (b, 0, 0))] + cache_specs,
        out_specs=[out_spec] * (2 * N_GROUPS),
        out_shape=[out_shape] * (2 * N_GROUPS),
        compiler_params=_params(("parallel",)),
        name="attn_sample",
    )(z3, *cache_args)
    return [o.reshape(nreq * SAMPLE_ROWS, GROUP_W) for o in outs]


def _ssm_operators(lam_re, lam_im, log_dt, b_re, b_im, c_re, c_im, chunk):
    f32 = jnp.float32
    hi = lax.Precision.HIGHEST
    lam_re, lam_im = lam_re.astype(f32), lam_im.astype(f32)
    dt = jnp.exp(log_dt.astype(f32))[:, None]
    decay = jnp.exp(lam_re * dt)
    a_re = decay * jnp.cos(lam_im * dt)
    a_im = decay * jnp.sin(lam_im * dt)
    den = lam_re * lam_re + lam_im * lam_im
    f_re = ((a_re - 1.0) * lam_re + a_im * lam_im) / den
    f_im = (a_im * lam_re - (a_re - 1.0) * lam_im) / den
    b_re, b_im = b_re.astype(f32), b_im.astype(f32)
    bb_re = f_re[..., None] * b_re - f_im[..., None] * b_im
    bb_im = f_re[..., None] * b_im + f_im[..., None] * b_re
    c_re, c_im = c_re.astype(f32), c_im.astype(f32)

    def power(p):
        p = jnp.asarray(p, f32)[..., None, None]
        mag = jnp.exp(lam_re * dt * p)
        ang = lam_im * dt * p
        return mag * jnp.cos(ang), mag * jnp.sin(ang)

    n_g = lam_re.shape[0]
    ap_re, ap_im = power(jnp.arange(chunk + 1))
    abb_re = ap_re[:chunk, :, :, None] * bb_re - ap_im[:chunk, :, :, None] * bb_im
    abb_im = ap_re[:chunk, :, :, None] * bb_im + ap_im[:chunk, :, :, None] * bb_re
    kern = (jnp.einsum('gcn,tgnd->tgcd', c_re, abb_re, precision=hi)
            - jnp.einsum('gcn,tgnd->tgcd', c_im, abb_im, precision=hi))
    kern = jnp.concatenate([kern, jnp.zeros_like(kern[:1])], axis=0)
    lag = jnp.arange(chunk)[:, None] - jnp.arange(chunk)[None, :]
    m_op = kern[jnp.where(lag >= 0, lag, chunk)]
    m_op = m_op.transpose(2, 0, 3, 1, 4).reshape(n_g, chunk * SSM_C, chunk * SSM_C)
    rev = chunk - 1 - jnp.arange(chunk)
    p_re = abb_re[rev].transpose(1, 2, 0, 3).reshape(n_g, SSM_N, chunk * SSM_C)
    p_im = abb_im[rev].transpose(1, 2, 0, 3).reshape(n_g, SSM_N, chunk * SSM_C)
    p_op = jnp.concatenate([p_re, p_im], axis=1)
    a1_re, a1_im = ap_re[1:], ap_im[1:]
    q_re = c_re[None] * a1_re[:, :, None, :] - c_im[None] * a1_im[:, :, None, :]
    q_im = -(c_re[None] * a1_im[:, :, None, :] + c_im[None] * a1_re[:, :, None, :])
    q_op = jnp.concatenate([q_re, q_im], axis=-1).transpose(1, 0, 2, 3).reshape(n_g, chunk * SSM_C, 2 * SSM_N)
    return m_op, p_op, q_op, power


def _u_proj_kernel(x_ref, w_ref, u_ref, slab):
    n_slab, n_tok, _ = slab.shape
    n_chunk = n_tok // SSM_CHUNK
    groups_per_slab = LANES // SSM_C
    for j2 in range(n_slab // 2):
        zt = _dot(_mxu(x_ref[...]), _mxu(w_ref[:, 2 * j2 * LANES:2 * (j2 + 1) * LANES]))
        slab[2 * j2] = zt[:, :LANES]
        slab[2 * j2 + 1] = zt[:, LANES:]
    for j in range(n_slab):
        for s in range(SSM_CHUNK):
            piece = slab[j, pl.ds(s, n_chunk, stride=SSM_CHUNK), :].T
            for gg in range(groups_per_slab):
                u_ref[j * groups_per_slab + gg, s * SSM_C:(s + 1) * SSM_C, :] = _mxu(piece[gg * SSM_C:(gg + 1) * SSM_C, :])


def _u_proj(x, w, layer, col0, n_g):
    n_tok, k = x.shape
    blk_tok = LANES * SSM_CHUNK
    halves = 2
    cols = n_g * SSM_C // halves
    assert n_tok % blk_tok == 0 and cols % (2 * LANES) == 0 and col0 % cols == 0
    lc = SSM_CHUNK * SSM_C
    return pl.pallas_call(
        _u_proj_kernel,
        grid=(n_tok // blk_tok, halves),
        in_specs=[pl.BlockSpec((blk_tok, k), lambda i, hf: (i, 0)),
                  pl.BlockSpec((None, k, cols), lambda i, hf: (layer, 0, col0 // cols + hf))],
        out_specs=pl.BlockSpec((n_g // halves, lc, LANES), lambda i, hf: (hf, 0, i)),
        out_shape=jax.ShapeDtypeStruct((n_g, lc, n_tok // SSM_CHUNK), MXU_DTYPE),
        scratch_shapes=[pltpu.VMEM((cols // LANES, blk_tok, LANES), jnp.float32)],
        compiler_params=_params(("parallel", "arbitrary")),
        name="u_proj",
    )(x, w)


def _ssm_prompt_kernel(u_ref, m_ref, p_ref, q_ref, pw_ref, y_ref, h_ref, ybuf, *, chunks_per_seq, n_steps):
    n_cols = u_ref.shape[2]
    lane = lax.broadcasted_iota(jnp.int32, (SSM_N, n_cols), 1) % chunks_per_seq
    for gi in range(u_ref.shape[0]):
        u = u_ref[gi]
        xc = _dot(p_ref[gi], u)
        h_re, h_im = xc[:SSM_N], xc[SSM_N:]
        for j in range(n_steps):
            sh = 1 << j
            a_re = pw_ref[gi, :, j:j + 1]
            a_im = pw_ref[gi, :, n_steps + j:n_steps + j + 1]
            keep = lane >= sh
            s_re = jnp.where(keep, pltpu.roll(h_re, sh, 1), 0.0)
            s_im = jnp.where(keep, pltpu.roll(h_im, sh, 1), 0.0)
            h_re, h_im = h_re + a_re * s_re - a_im * s_im, h_im + a_re * s_im + a_im * s_re
        h_ref[gi, 0:SSM_N, :] = h_re
        h_ref[gi, SSM_N:, :] = h_im
        first = lane >= 1
        hp = jnp.concatenate([jnp.where(first, pltpu.roll(h_re, 1, 1), 0.0),
                              jnp.where(first, pltpu.roll(h_im, 1, 1), 0.0)], axis=0)
        ybuf[gi] = _dot(m_ref[gi], u) + _dot(q_ref[gi], _mxu(hp))
    for t in range(SSM_CHUNK):
        rows = jnp.concatenate([ybuf[gi, t * SSM_C:(t + 1) * SSM_C, :] for gi in range(u_ref.shape[0])], axis=0)
        for cb in range(n_cols // LANES):
            y_ref[pl.ds(cb * LANES * SSM_CHUNK + t, LANES, stride=SSM_CHUNK), :] = rows[:, cb * LANES:(cb + 1) * LANES].T


def _ssm_prompt(ut, bsz, seq, ops):
    m_op, p_op, q_op, power = ops
    n_g = m_op.shape[0]
    chunks_per_seq = seq // SSM_CHUNK
    n_cols = bsz * chunks_per_seq
    n_steps = int(math.log2(chunks_per_seq))
    assert (1 << n_steps) == chunks_per_seq and n_cols % LANES == 0
    lc = SSM_CHUNK * SSM_C
    pw_re, pw_im = power(SSM_CHUNK * (2 ** jnp.arange(n_steps)))
    pw = jnp.concatenate([pw_re, pw_im], axis=0).transpose(1, 2, 0)
    gb = LANES // SSM_C
    assert n_g % gb == 0
    y, h = pl.pallas_call(
        functools.partial(_ssm_prompt_kernel, chunks_per_seq=chunks_per_seq, n_steps=n_steps),
        grid=(n_g // gb,),
        in_specs=[pl.BlockSpec((gb, lc, n_cols), lambda i: (i, 0, 0)),
                  pl.BlockSpec((gb, lc, lc), lambda i: (i, 0, 0)),
                  pl.BlockSpec((gb, 2 * SSM_N, lc), lambda i: (i, 0, 0)),
                  pl.BlockSpec((gb, lc, 2 * SSM_N), lambda i: (i, 0, 0)),
                  pl.BlockSpec((gb, SSM_N, 2 * n_steps), lambda i: (i, 0, 0))],
        out_specs=[pl.BlockSpec((bsz * seq, LANES), lambda i: (0, i)),
                   pl.BlockSpec((gb, 2 * SSM_N, n_cols), lambda i: (i, 0, 0))],
        out_shape=[jax.ShapeDtypeStruct((bsz * seq, n_g * SSM_C), jnp.float32),
                   jax.ShapeDtypeStruct((n_g, 2 * SSM_N, n_cols), jnp.float32)],
        scratch_shapes=[pltpu.VMEM((gb, lc, n_cols), jnp.float32)],
        compiler_params=_params(("parallel",)),
        name="ssm_prompt",
    )(ut, _mxu(m_op), _mxu(p_op), _mxu(q_op), pw)
    h_last = h.reshape(n_g, 2, SSM_N, bsz, chunks_per_seq)[..., -1].transpose(3, 0, 2, 1)
    return y, h_last


def _ssm_sample_kernel(u_ref, h0_ref, m_ref, p_ref, q_ref, a_ref, y_ref, h_ref):
    for gi in range(u_ref.shape[0]):
        u = u_ref[gi]
        h0 = h0_ref[gi]
        h0_re, h0_im = h0[:SSM_N], h0[SSM_N:]
        a_re = a_ref[gi, :, 0:1]
        a_im = a_ref[gi, :, 1:2]
        xc = _dot(p_ref[gi], u)
        h_ref[gi, 0:SSM_N, :] = a_re * h0_re - a_im * h0_im + xc[:SSM_N]
        h_ref[gi, SSM_N:, :] = a_re * h0_im + a_im * h0_re + xc[SSM_N:]
        y_ref[gi] = _dot(m_ref[gi], u) + _dot(q_ref[gi], _mxu(h0))


def _ssm_sample(u, h0, nreq, n_new, ops):
    m_op, p_op, q_op, power = ops
    n_g = m_op.shape[0]
    lc = n_new * SSM_C
    ut = u.reshape(nreq, SAMPLE_ROWS, n_g, SSM_C)[:, :n_new].transpose(2, 1, 3, 0).reshape(n_g, lc, nreq)
    ut = _mxu(jnp.pad(ut, ((0, 0), (0, 0), (0, LANES - nreq))))
    h0t = h0.astype(jnp.float32).transpose(1, 3, 2, 0).reshape(n_g, 2 * SSM_N, nreq)
    h0t = jnp.pad(h0t, ((0, 0), (0, 0), (0, LANES - nreq)))
    a_re, a_im = power(n_new)
    a_op = jnp.stack([a_re, a_im], axis=-1)
    gb = SSM_GROUP_BLOCK
    y, h = pl.pallas_call(
        _ssm_sample_kernel,
        grid=(n_g // gb,),
        in_specs=[pl.BlockSpec((gb, lc, LANES), lambda i: (i, 0, 0)),
                  pl.BlockSpec((gb, 2 * SSM_N, LANES), lambda i: (i, 0, 0)),
                  pl.BlockSpec((gb, lc, lc), lambda i: (i, 0, 0)),
                  pl.BlockSpec((gb, 2 * SSM_N, lc), lambda i: (i, 0, 0)),
                  pl.BlockSpec((gb, lc, 2 * SSM_N), lambda i: (i, 0, 0)),
                  pl.BlockSpec((gb, SSM_N, 2), lambda i: (i, 0, 0))],
        out_specs=[pl.BlockSpec((gb, lc, LANES), lambda i: (i, 0, 0)),
                   pl.BlockSpec((gb, 2 * SSM_N, LANES), lambda i: (i, 0, 0))],
        out_shape=[jax.ShapeDtypeStruct((n_g, lc, LANES), jnp.float32),
                   jax.ShapeDtypeStruct((n_g, 2 * SSM_N, LANES), jnp.float32)],
        compiler_params=_params(("parallel",)),
        name="ssm_sample",
    )(ut, h0t, _mxu(m_op), _mxu(p_op), _mxu(q_op), a_op)
    y = y[:, :, :nreq].reshape(n_g, n_new, SSM_C, nreq).transpose(3, 1, 0, 2).reshape(nreq, n_new, n_g * SSM_C)
    y = jnp.pad(y, ((0, 0), (0, SAMPLE_ROWS - n_new), (0, 0))).reshape(nreq * SAMPLE_ROWS, n_g * SSM_C)
    h_new = h[:, :, :nreq].reshape(n_g, 2, SSM_N, nreq).transpose(3, 0, 2, 1)
    return y, h_new


def _mix_kernel(o0_ref, l0_ref, o1_ref, l1_ref, o2_ref, l2_ref, ys_ref, u_ref, ga_ref, gs_ref, dsk_ref, wglu_ref,
                bglu_ref, wap_ref, wsp_ref, o_ref, attn_buf):
    outs = (o0_ref, o1_ref, o2_ref)
    for j in range(HEADS):
        sl = slice(j * HD, (j + 1) * HD)
        lses = [l[:, sl] for l in (l0_ref, l1_ref, l2_ref)]
        top = jnp.maximum(jnp.maximum(lses[0], lses[1]), lses[2])
        ws = [jnp.exp(l - top) for l in lses]
        num = ws[0] * outs[0][:, sl] + ws[1] * outs[1][:, sl] + ws[2] * outs[2][:, sl]
        attn_buf[:, sl] = _mxu(num / (ws[0] + ws[1] + ws[2]))
    y = jax.nn.gelu(ys_ref[...] + dsk_ref[...] * u_ref[...])
    ssm = y * jax.nn.sigmoid(_dot(_mxu(y), wglu_ref[...]) + bglu_ref[...])
    merged = (jax.nn.sigmoid(ga_ref[...]) * _dot(attn_buf[...], wap_ref[...])
              + jax.nn.sigmoid(gs_ref[...]) * _dot(_mxu(ssm), wsp_ref[...]))
    o_ref[...] = _mxu(merged)


def _mix(parts, ys, z, d_skip, w_glu, b_glu, w_ap, w_sp, tm):
    m = ys.shape[0]
    ssm_w = ys.shape[1]
    d_model = w_ap.shape[1]
    u_off, ga_off = 3 * ATTN_W, 3 * ATTN_W + ssm_w
    gs_off = ga_off + d_model
    assert u_off % ssm_w == 0 and ga_off % d_model == 0 and m % tm == 0
    row = lambda w: pl.BlockSpec((tm, w), lambda i: (i, 0))
    const = lambda a: pl.BlockSpec(a.shape, lambda i: (0,) * a.ndim)
    d_skip, b_glu = d_skip.reshape(1, -1), b_glu.reshape(1, -1)
    return pl.pallas_call(
        _mix_kernel,
        grid=(m // tm,),
        in_specs=[row(GROUP_W)] * (2 * N_GROUPS) + [row(ssm_w),
                  pl.BlockSpec((tm, ssm_w), lambda i: (i, u_off // ssm_w)),
                  pl.BlockSpec((tm, d_model), lambda i: (i, ga_off // d_model)),
                  pl.BlockSpec((tm, d_model), lambda i: (i, gs_off // d_model)),
                  const(d_skip), const(w_glu), const(b_glu), const(w_ap), const(w_sp)],
        out_specs=row(d_model),
        out_shape=jax.ShapeDtypeStruct((m, d_model), MXU_DTYPE),
        scratch_shapes=[pltpu.VMEM((tm, GROUP_W), MXU_DTYPE)],
        compiler_params=_params(("parallel",)),
        name="mix",
    )(*parts, ys, z, z, z, d_skip, w_glu, b_glu, w_ap, w_sp)


def _layer_norm(h, g, b):
    mu = jnp.mean(h, axis=-1, keepdims=True)
    d = h - mu
    var = jnp.mean(d * d, axis=-1, keepdims=True)
    return d * lax.rsqrt(var + LN_EPS) * g + b


def _split_hi_lo(a):
    hi = _mxu(a)
    return hi, _mxu(a - hi.astype(jnp.float32))


def _out_kernel(m_ref, x_ref, wo_ref, g_ref, b_ref, rw_ref, rb_ref, x1_ref, lg_ref, *, alpha):
    h = alpha * x_ref[...] + _dot(m_ref[...], wo_ref[...])
    x1 = _layer_norm(h, g_ref[...], b_ref[...])
    x1_ref[...] = x1
    hi, lo = _split_hi_lo(x1)
    both = _dot(hi, rw_ref[...])
    lg_ref[...] = both[:, :LANES] + both[:, LANES:] + _dot(lo, rw_ref[:, :LANES]) + rb_ref[...]


def _out_proj(merged, x, w_o, g, b, rw, rb, alpha, tm):
    m, d_model = x.shape
    assert m % tm == 0
    row = lambda w: pl.BlockSpec((tm, w), lambda i: (i, 0))
    const = lambda a: pl.BlockSpec(a.shape, lambda i: (0,) * a.ndim)
    g, b = g.reshape(1, -1), b.reshape(1, -1)
    return pl.pallas_call(
        functools.partial(_out_kernel, alpha=alpha),
        grid=(m // tm,),
        in_specs=[row(d_model), row(d_model), const(w_o), const(g), const(b), const(rw), const(rb)],
        out_specs=[row(d_model), row(LANES)],
        out_shape=[jax.ShapeDtypeStruct((m, d_model), jnp.float32),
                   jax.ShapeDtypeStruct((m, LANES), jnp.float32)],
        compiler_params=_params(("parallel",)),
        name="out_proj",
    )(merged, x, w_o, g, b, rw, rb)


def _moe_kernel(ce_ref, cr_ref, cn_ref, ct_ref, cm_ref, tot_ref, tok_ref, x_hbm, wg_ref, wl_ref, wd_ref, bg_ref,
                bl_ref, bd_ref, y_hbm, xbuf, acc, wgb, wlb, wdb, sem_in, sem_out, *, n_ff_tiles, n_sub_total):
    c = pl.program_id(0)
    j = pl.program_id(1)
    nsb = cn_ref[c]
    row0 = cr_ref[c]
    tok0 = ct_ref[c]
    last_real = cm_ref[c] - 1

    def sub_rows(s):
        return pl.ds(pl.multiple_of(s * MOE_SUB, MOE_SUB), MOE_SUB)

    def x_copy(i):
        tok = tok_ref[tok0 + jnp.minimum(i, last_real)]
        return pltpu.make_async_copy(x_hbm.at[pl.ds(tok, 1)], xbuf.at[pl.ds(i, 1)], sem_in)

    def y_copy(s, first_sub):
        return pltpu.make_async_copy(acc.at[sub_rows(s)], y_hbm.at[sub_rows(first_sub + s)], sem_out)

    def for_subs(n, fn):
        def body(s, carry):
            fn(s)
            return carry
        lax.fori_loop(0, n, body, 0)

    def for_rows(n_sub, fn):
        def body(g, carry):
            for k in range(MOE_ROW_UNROLL):
                fn(g * MOE_ROW_UNROLL + k)
            return carry
        lax.fori_loop(0, n_sub * (MOE_SUB // MOE_ROW_UNROLL), body, 0)

    @pl.when((j == 0) & (nsb > 0))
    def _():
        for_rows(nsb, lambda i: x_copy(i).start())

    @pl.when((j == 0) & (c > 0))
    def _():
        prev = jnp.maximum(c - 1, 0)
        for_subs(cn_ref[prev], lambda s: y_copy(s, cr_ref[prev]).wait())

    @pl.when((j == 0) & (nsb > 0))
    def _():
        def init(s):
            acc[sub_rows(s), :] = jnp.broadcast_to(bd_ref[...], (MOE_SUB, acc.shape[1]))

        for_subs(nsb, init)
        for_rows(nsb, lambda i: x_copy(i).wait())

    @pl.when(nsb > 0)
    def _():
        wgb[...] = _mxu(wg_ref[...])
        wlb[...] = _mxu(wl_ref[...])
        wdb[...] = _mxu(wd_ref[...])

        def sub(s):
            rows = sub_rows(s)
            xs = _mxu(xbuf[rows, :])
            glu = jnp.minimum(_dot(xs, wgb[...]) + bg_ref[...], SWIGLU_LIMIT)
            lin = jnp.clip(_dot(xs, wlb[...]) + bl_ref[...], -SWIGLU_LIMIT, SWIGLU_LIMIT)
            act = glu * jax.nn.sigmoid(SWIGLU_ALPHA * glu) * (lin + 1.0)
            acc[rows, :] += _dot(_mxu(act), wdb[...])

        done = 0
        for width in MOE_GROUP_WIDTHS:
            def group(p, base=done, width=width):
                for k in range(width):
                    sub(base + width * p + k)

            trips = (nsb - done) // width
            for_subs(trips, group)
            done = done + trips * width

    @pl.when((j == n_ff_tiles - 1) & (nsb > 0))
    def _():
        for_subs(nsb, lambda s: y_copy(s, row0).start())

    @pl.when((c == pl.num_programs(0) - 1) & (j == n_ff_tiles - 1))
    def _():
        for_subs(nsb, lambda s: y_copy(s, row0).wait())
        first = tot_ref[0] // MOE_SUB
        acc[0:MOE_SUB, :] = jnp.zeros((MOE_SUB, acc.shape[1]), acc.dtype)

        def z_copy(s):
            return pltpu.make_async_copy(acc.at[pl.ds(0, MOE_SUB)], y_hbm.at[sub_rows(first + s)], sem_out)

        for_subs(n_sub_total - first, lambda s: z_copy(s).start())
        for_subs(n_sub_total - first, lambda s: z_copy(s).wait())


def _moe_experts(x, route, w_gu, b_gu, w_dn, b_dn, layer):
    chunk_expert = route["chunk_expert"]
    n_slots = route["n_slots"]
    d_model = x.shape[1]
    d_ff = w_dn.shape[2]
    tf = MOE_FF_TILE
    n_ff_tiles = d_ff // tf
    n_chunks = chunk_expert.shape[0]
    rows = MOE_SUB * MOE_CHUNK_SUBS
    n_exp = w_gu.shape[1]
    b_gu4 = b_gu.reshape(b_gu.shape[0], n_exp, 1, 2 * d_ff)
    b_dn4 = b_dn.reshape(b_dn.shape[0], n_exp, 1, d_model)

    def ff(j, c, cn):
        return jnp.where(cn[c] > 0, j, n_ff_tiles - 1)

    def wspec(shape, index):
        return pl.BlockSpec((None, None) + shape,
                            lambda c, j, ce, cr, cn, *_: (layer, ce[c]) + index(ff(j, c, cn)))

    grid_spec = pltpu.PrefetchScalarGridSpec(
        num_scalar_prefetch=7,
        grid=(n_chunks, n_ff_tiles),
        in_specs=[
            pl.BlockSpec(memory_space=pl.ANY),
            wspec((d_model, tf), lambda t: (0, t)),
            wspec((d_model, tf), lambda t: (0, n_ff_tiles + t)),
            wspec((tf, d_model), lambda t: (t, 0)),
            wspec((1, tf), lambda t: (0, t)),
            wspec((1, tf), lambda t: (0, n_ff_tiles + t)),
            wspec((1, d_model), lambda t: (0, 0)),
        ],
        out_specs=pl.BlockSpec(memory_space=pl.ANY),
        scratch_shapes=[
            pltpu.VMEM((rows, d_model), jnp.float32),
            pltpu.VMEM((rows, d_model), jnp.float32),
            pltpu.VMEM((d_model, tf), MXU_DTYPE),
            pltpu.VMEM((d_model, tf), MXU_DTYPE),
            pltpu.VMEM((tf, d_model), MXU_DTYPE),
            pltpu.SemaphoreType.DMA(()),
            pltpu.SemaphoreType.DMA(()),
        ],
    )
    return pl.pallas_call(
        functools.partial(_moe_kernel, n_ff_tiles=n_ff_tiles, n_sub_total=n_slots // MOE_SUB),
        grid_spec=grid_spec,
        out_shape=jax.ShapeDtypeStruct((n_slots, d_model), jnp.float32),
        compiler_params=_params(("arbitrary", "arbitrary")),
        name="moe_experts",
    )(chunk_expert, route["chunk_row"], route["chunk_subs"], route["chunk_tok"], route["chunk_real"],
      route["total_rows"], route["tok_sorted"], x, w_gu, w_gu, w_dn, b_gu4, b_gu4, b_dn4)


def _route(logits, n_exp):
    n_tok = logits.shape[0]
    top_val, top_idx = lax.top_k(logits[:, :n_exp], TOP_K)
    gate = jax.nn.softmax(top_val, axis=-1)
    n_assign = n_tok * TOP_K
    n_slots = -(-(n_assign + n_exp * (MOE_SUB - 1)) // MOE_SUB) * MOE_SUB
    flat_e = top_idx.reshape(-1)
    onehot = (flat_e[:, None] == jnp.arange(n_exp, dtype=flat_e.dtype)[None, :]).astype(jnp.int32)
    before = jnp.cumsum(onehot, axis=0) - onehot
    rank = jnp.take_along_axis(before, flat_e[:, None], axis=1)[:, 0]
    counts = onehot.sum(axis=0)
    padded = (counts + MOE_SUB - 1) // MOE_SUB * MOE_SUB
    pad_end = jnp.cumsum(padded)
    pad_start = pad_end - padded
    dest = (pad_start[flat_e] + rank).astype(jnp.int32)
    tok_sorted = (jnp.argsort(flat_e, stable=True) // TOP_K).astype(jnp.int32)
    grp_start = jnp.cumsum(counts) - counts
    rows = MOE_SUB * MOE_CHUNK_SUBS
    n_chunks = n_slots // rows + n_exp + 1
    per_exp = (padded + rows - 1) // rows
    chunk_end = jnp.cumsum(per_exp)
    cid = jnp.arange(n_chunks)
    exp_of = jnp.minimum(jnp.searchsorted(chunk_end, cid, side='right'), n_exp - 1)
    within = cid - (chunk_end - per_exp)[exp_of]
    live = cid < chunk_end[-1]
    row0 = pad_start[exp_of] + within * rows
    subs = jnp.clip(padded[exp_of] - within * rows, 0, rows) // MOE_SUB
    last_exp = exp_of[jnp.maximum(chunk_end[-1] - 1, 0)]
    as_i32 = lambda a: a.astype(jnp.int32)
    route = dict(
        n_slots=n_slots,
        chunk_expert=as_i32(jnp.where(live, exp_of, last_exp)),
        chunk_row=as_i32(jnp.where(live, row0, 0) // MOE_SUB),
        chunk_subs=as_i32(jnp.where(live, subs, 0)),
        chunk_tok=as_i32(jnp.where(live, grp_start[exp_of] + within * rows, 0)),
        chunk_real=as_i32(jnp.where(live, jnp.clip(counts[exp_of] - within * rows, 1, rows), 1)),
        total_rows=as_i32(pad_end[-1:]),
        tok_sorted=tok_sorted,
    )
    return gate, dest, route


def _combine_kernel(x_ref, y0_ref, y1_ref, y2_ref, y3_ref, gate_ref, g_ref, b_ref, o_ref, ob_ref, *, alpha):
    h = alpha * x_ref[...]
    for k, y_ref in enumerate((y0_ref, y1_ref, y2_ref, y3_ref)):
        h = h + gate_ref[:, k:k + 1] * y_ref[...]
    x2 = _layer_norm(h, g_ref[...], b_ref[...])
    o_ref[...] = x2
    ob_ref[...] = _mxu(x2)


def _combine(x1, yb_k, gate, g, b, alpha, tm):
    m, d_model = x1.shape
    assert m % tm == 0 and len(yb_k) == TOP_K
    row = lambda w: pl.BlockSpec((tm, w), lambda i: (i, 0))
    const = lambda a: pl.BlockSpec(a.shape, lambda i: (0,) * a.ndim)
    g, b = g.reshape(1, -1), b.reshape(1, -1)
    gate = jnp.pad(gate, ((0, 0), (0, LANES - TOP_K)))
    return pl.pallas_call(
        functools.partial(_combine_kernel, alpha=alpha),
        grid=(m // tm,),
        in_specs=[row(d_model)] * (1 + TOP_K) + [row(LANES), const(g), const(b)],
        out_specs=[row(d_model), row(d_model)],
        out_shape=[jax.ShapeDtypeStruct((m, d_model), jnp.float32),
                   jax.ShapeDtypeStruct((m, d_model), MXU_DTYPE)],
        compiler_params=_params(("parallel",)),
        name="combine",
    )(x1, *yb_k, gate, g, b)


def _forward(x_prompt, x_sample, cache_kv_w128, cache_kv_w512, cache_kv_w2048, state_ssm, w_in, w_attn_proj,
             ssm_lambda_re, ssm_lambda_im, ssm_log_dt, ssm_b_re, ssm_b_im, ssm_c_re, ssm_c_im, ssm_d, w_glu,
             b_glu, w_ssm_proj, w_o, ln1_g, ln1_b, router_w, router_b, w_gate_up, b_gate_up, w_down, b_down,
             ln2_g, ln2_b):
    caches = (cache_kv_w128, cache_kv_w512, cache_kv_w2048)
    bsz, seq, d_model = x_prompt.shape
    nreq, n_new, _ = x_sample.shape
    depth = w_in.shape[0]
    n_exp = router_w.shape[2]
    alpha = (2 * depth) ** 0.25
    n_p = bsz * seq
    n_s = nreq * SAMPLE_ROWS
    assert n_new <= SAMPLE_ROWS

    xp = x_prompt.reshape(n_p, d_model)
    xs = jnp.pad(x_sample, ((0, 0), (0, SAMPLE_ROWS - n_new), (0, 0))).reshape(n_s, d_model)
    xpb, xsb = _mxu(xp), _mxu(xs)
    kv_p = [[] for _ in ATTN_PATTERNS]
    kv_s = [[] for _ in ATTN_PATTERNS]
    h_p, h_s = [], []
    tm_in = 1024 if n_p % 1024 == 0 else 256

    for layer in range(depth):
        w_glu_l, w_ap_l, w_sp_l, w_o_l = (_mxu(w[layer]) for w in (w_glu, w_attn_proj, w_ssm_proj, w_o))
        rw_l = jnp.concatenate(_split_hi_lo(jnp.pad(router_w[layer], ((0, 0), (0, LANES - n_exp)))), axis=1)
        rb_l = jnp.pad(router_b[layer], (0, LANES - n_exp)).reshape(1, LANES)
        lam = (ssm_lambda_re[layer], ssm_lambda_im[layer], ssm_log_dt[layer], ssm_b_re[layer], ssm_b_im[layer],
               ssm_c_re[layer], ssm_c_im[layer])
        ssm_w = ssm_d.shape[1]
        u_lo = 3 * ATTN_W

        zp = _matmul(xpb, w_in, layer, tm_in, 1024)
        z3 = zp.reshape(bsz, seq, -1)
        parts_p = [a for g in range(N_GROUPS) for a in _attn_prompt(z3, g)]
        ut = _u_proj(xpb, w_in, layer, u_lo, ssm_w // SSM_C)
        ys_p, h_last = _ssm_prompt(ut, bsz, seq, _ssm_operators(*lam, SSM_CHUNK))
        h_p.append(h_last)
        for g, (window, _) in enumerate(ATTN_PATTERNS):
            keep = min(window, seq)
            k_g = z3[:, seq - keep:, ATTN_W + g * GROUP_W:ATTN_W + (g + 1) * GROUP_W]
            v_g = z3[:, seq - keep:, 2 * ATTN_W + g * GROUP_W:2 * ATTN_W + (g + 1) * GROUP_W]
            kv_p[g].append(jnp.stack([k_g, v_g], axis=2).reshape(bsz, keep, 2, HEADS, HD))
        mg_p = _mix(parts_p, ys_p, zp, ssm_d[layer], w_glu_l, b_glu[layer], w_ap_l, w_sp_l, 128)
        x1_p, lg_p = _out_proj(mg_p, xp, w_o_l, ln1_g[layer], ln1_b[layer], rw_l, rb_l, alpha, 256)

        zs = _matmul(xsb, w_in, layer, n_s, 1024)
        parts_s = _attn_sample(zs, caches, layer, nreq)
        ys_s, h_new = _ssm_sample(zs[:, u_lo:u_lo + ssm_w], state_ssm[layer], nreq, n_new,
                                  _ssm_operators(*lam, n_new))
        h_s.append(h_new)
        zs3 = zs.reshape(nreq, SAMPLE_ROWS, -1)[:, :n_new]
        for g in range(N_GROUPS):
            k_g = zs3[:, :, ATTN_W + g * GROUP_W:ATTN_W + (g + 1) * GROUP_W]
            v_g = zs3[:, :, 2 * ATTN_W + g * GROUP_W:2 * ATTN_W + (g + 1) * GROUP_W]
            kv_s[g].append(jnp.stack([k_g, v_g], axis=2).reshape(nreq, n_new, 2, HEADS, HD))
        mg_s = _mix(parts_s, ys_s, zs, ssm_d[layer], w_glu_l, b_glu[layer], w_ap_l, w_sp_l, n_s)
        x1_s, lg_s = _out_proj(mg_s, xs, w_o_l, ln1_g[layer], ln1_b[layer], rw_l, rb_l, alpha, n_s)

        logits = jnp.concatenate([lg_p, lg_s], axis=0)
        gate, dest, route = _route(logits, n_exp)
        yb = _moe_experts(jnp.concatenate([x1_p, x1_s], axis=0), route, w_gate_up, b_gate_up, w_down, b_down, layer)
        dest = dest.reshape(n_p + n_s, TOP_K)
        xp, xpb = _combine(x1_p, [yb[dest[:n_p, k]] for k in range(TOP_K)], gate[:n_p],
                           ln2_g[layer], ln2_b[layer], alpha, 256)
        xs, xsb = _combine(x1_s, [yb[dest[n_p:, k]] for k in range(TOP_K)], gate[n_p:],
                           ln2_g[layer], ln2_b[layer], alpha, n_s)

    y_prompt = xp.reshape(bsz, seq, d_model)
    y_sample = xs.reshape(nreq, SAMPLE_ROWS, d_model)[:, :n_new]
    kv_s_out = [jnp.concatenate([c.astype(rows[0].dtype), jnp.stack(rows)], axis=2)[:, :, -c.shape[2]:]
                for rows, c in zip(kv_s, caches)]
    return (y_prompt, y_sample,
            jnp.stack(kv_p[0]), jnp.stack(kv_p[1]), jnp.stack(kv_p[2]), jnp.stack(h_p),
            kv_s_out[0], kv_s_out[1], kv_s_out[2], jnp.stack(h_s))


_forward_jit = jax.jit(_forward)


def kernel(x_prompt, x_sample, cache_kv_w128, cache_kv_w512, cache_kv_w2048, state_ssm, w_in, w_attn_proj,
           ssm_lambda_re, ssm_lambda_im, ssm_log_dt, ssm_b_re, ssm_b_im, ssm_c_re, ssm_c_im, ssm_d, w_glu, b_glu,
           w_ssm_proj, w_o, ln1_g, ln1_b, router_w, router_b, w_gate_up, b_gate_up, w_down, b_down, ln2_g, ln2_b):
    return _forward_jit(x_prompt, x_sample, cache_kv_w128, cache_kv_w512, cache_kv_w2048, state_ssm, w_in,
                        w_attn_proj, ssm_lambda_re, ssm_lambda_im, ssm_log_dt, ssm_b_re, ssm_b_im, ssm_c_re,
                        ssm_c_im, ssm_d, w_glu, b_glu, w_ssm_proj, w_o, ln1_g, ln1_b, router_w, router_b,
                        w_gate_up, b_gate_up, w_down, b_down, ln2_g, ln2_b)
```

```python
import functools
import math

import jax
import jax.numpy as jnp
from jax import lax
from jax.experimental import pallas as pl
from jax.experimental.pallas import tpu as pltpu

ATTN_PATTERNS = ((128, 1), (512, 4), (2048, 16))
N_GROUPS = len(ATTN_PATTERNS)
HEADS = 4
HD = 128
GROUP_W = HEADS * HD
ATTN_W = N_GROUPS * GROUP_W
N_HEADS = N_GROUPS * HEADS
ALIBI_SLOPES = tuple(2.0 ** (-8.0 * (h + 1) / N_HEADS) for h in range(N_HEADS))
QB = 128
ATTN_BLOCK = QB * max(d for _, d in ATTN_PATTERNS)
SSM_C = 16
SSM_N = 64
TOP_K = 4
SWIGLU_LIMIT = 7.0
SWIGLU_ALPHA = 1.702
LN_EPS = 1e-5

LANES = 128
SUBLANES = 8
VMEM_LIMIT_BYTES = 56 * 1024 * 1024

SSM_CHUNK = 16
SSM_GROUP_BLOCK = 8
SAMPLE_ROWS = 16
MOE_SUB = 256
MOE_CHUNK_SUBS = 6
MOE_FF_TILE = 512
MOE_VMEM_LIMIT_BYTES = 60 * 1024 * 1024
MOE_ROW_UNROLL = 8
MOE_GROUP_WIDTHS = (4, 2, 1)
MXU_DTYPE = jnp.bfloat16
NEG_BIG = -1e30


def _params(sem, vmem_limit=VMEM_LIMIT_BYTES):
    return pltpu.CompilerParams(dimension_semantics=sem, vmem_limit_bytes=vmem_limit)


def _mxu(a):
    return a.astype(MXU_DTYPE)


def _dot(a, b):
    return jnp.dot(a, b, preferred_element_type=jnp.float32)


def _mm_kernel(x_ref, w_ref, o_ref, wb):
    @pl.when(pl.program_id(1) == 0)
    def _():
        wb[...] = _mxu(w_ref[...])

    o_ref[...] = _dot(_mxu(x_ref[...]), wb[...])


def _matmul(x, w, layer, tm, tn):
    m, k = x.shape
    n = w.shape[2]
    assert m % tm == 0 and n % tn == 0
    return pl.pallas_call(
        _mm_kernel,
        grid=(n // tn, m // tm),
        in_specs=[pl.BlockSpec((tm, k), lambda j, i: (i, 0)),
                  pl.BlockSpec((None, k, tn), lambda j, i: (layer, 0, j))],
        out_specs=pl.BlockSpec((tm, tn), lambda j, i: (i, j)),
        out_shape=jax.ShapeDtypeStruct((m, n), jnp.float32),
        scratch_shapes=[pltpu.VMEM((k, tn), MXU_DTYPE)],
        compiler_params=_params(("arbitrary", "arbitrary")),
        name="in_proj",
    )(x, w)


def _attn_prompt_kernel(q_ref, kc_ref, kp_ref, vc_ref, vp_ref, o_ref, l_ref, kf, vf, *, tile, slopes, dilation):
    h = pl.program_id(1)
    n = pl.program_id(2)
    slope = jnp.float32(slopes[HEADS - 1])
    for hh in range(HEADS - 2, -1, -1):
        slope = jnp.where(h == hh, jnp.float32(slopes[hh]), slope)
    a = lax.broadcasted_iota(jnp.int32, (QB, 2 * QB), 0)
    c = lax.broadcasted_iota(jnp.int32, (QB, 2 * QB), 1)
    rel = a + QB - c
    band = (rel >= 0) & (rel <= QB)
    first_lo = jnp.where(n > 0, 0, QB)
    bias = slope * (rel * dilation).astype(jnp.float32)
    scale = HD ** -0.5

    def rows(r, start, count):
        if dilation == 1:
            return pl.ds(start, count)
        return pl.ds(start * dilation + r, count, stride=dilation)

    for r in range(dilation):
        kf[0:QB, :] = _mxu(kp_ref[rows(r, 0, QB), :])
        kf[QB:, :] = _mxu(kc_ref[rows(r, 0, tile), :])
        vf[0:QB, :] = _mxu(vp_ref[rows(r, 0, QB), :])
        vf[QB:, :] = _mxu(vc_ref[rows(r, 0, tile), :])
        for i in range(tile // QB):
            valid = (band & (c >= first_lo)) if i == 0 else band
            q = _mxu(q_ref[rows(r, i * QB, QB), :])
            k = kf[i * QB:(i + 2) * QB, :]
            v = vf[i * QB:(i + 2) * QB, :]
            s = lax.dot_general(q, k, (((1,), (1,)), ((), ())), preferred_element_type=jnp.float32) * scale
            s = jnp.where(valid, s - bias, NEG_BIG)
            m = jnp.max(s, axis=-1, keepdims=True)
            p = jnp.exp(s - m)
            l = jnp.sum(p, axis=-1, keepdims=True)
            o_ref[rows(r, i * QB, QB), :] = _dot(_mxu(p), v) / l
            l_ref[rows(r, i * QB, QB), :] = jnp.broadcast_to(m + jnp.log(l), (QB, HD))


def _attn_prompt(z3, g):
    window, dilation = ATTN_PATTERNS[g]
    assert window // dilation == QB
    bsz, seq, zc = z3.shape
    blk = ATTN_BLOCK
    tile = blk // dilation
    assert seq % blk == 0 and zc % HD == 0
    qcol, kcol, vcol = g * HEADS, (ATTN_W // HD) + g * HEADS, 2 * (ATTN_W // HD) + g * HEADS
    sub = blk // (QB * dilation)

    def cur(col):
        return pl.BlockSpec((None, blk, HD), lambda b, h, n: (b, n, col + h))

    def prev(col):
        return pl.BlockSpec((None, QB * dilation, HD), lambda b, h, n: (b, jnp.maximum(n * sub - 1, 0), col + h))

    slopes = ALIBI_SLOPES[g * HEADS:(g + 1) * HEADS]
    out_spec = pl.BlockSpec((None, blk, HD), lambda b, h, n: (b, n, h))
    out_shape = jax.ShapeDtypeStruct((bsz, seq, GROUP_W), jnp.float32)
    o, lse = pl.pallas_call(
        functools.partial(_attn_prompt_kernel, tile=tile, slopes=slopes, dilation=dilation),
        grid=(bsz, HEADS, seq // blk),
        in_specs=[cur(qcol), cur(kcol), prev(kcol), cur(vcol), prev(vcol)],
        out_specs=[out_spec, out_spec],
        out_shape=[out_shape, out_shape],
        scratch_shapes=[pltpu.VMEM((tile + QB, HD), MXU_DTYPE),
                        pltpu.VMEM((tile + QB, HD), MXU_DTYPE)],
        compiler_params=_params(("parallel", "parallel", "arbitrary")),
        name="attn_prompt_g%d" % g,
    )(z3, z3, z3, z3, z3)
    return o.reshape(bsz * seq, GROUP_W), lse.reshape(bsz * seq, GROUP_W)


def _attn_sample_kernel(z_ref, c0_ref, c1_ref, c2_ref, o0_ref, l0_ref, o1_ref, l1_ref, o2_ref, l2_ref):
    rows = SAMPLE_ROWS
    scale = HD ** -0.5
    for g, (c_ref, o_ref, l_ref) in enumerate(((c0_ref, o0_ref, l0_ref), (c1_ref, o1_ref, l1_ref),
                                                (c2_ref, o2_ref, l2_ref))):
        window, dilation = ATTN_PATTERNS[g]
        n_past = c_ref.shape[0] // (2 * HEADS)
        s_idx = lax.broadcasted_iota(jnp.int32, (rows, n_past), 0)
        i_idx = lax.broadcasted_iota(jnp.int32, (rows, n_past), 1)
        rel_c = n_past + s_idx - i_idx
        valid_c = (rel_c <= window) & ((rel_c & (dilation - 1)) == 0)
        dist_c = rel_c.astype(jnp.float32)
        s2 = lax.broadcasted_iota(jnp.int32, (rows, rows), 0)
        t2 = lax.broadcasted_iota(jnp.int32, (rows, rows), 1)
        rel_n = s2 - t2
        valid_n = (rel_n >= 0) & ((rel_n & (dilation - 1)) == 0)
        dist_n = rel_n.astype(jnp.float32)
        for h in range(HEADS):
            slope = ALIBI_SLOPES[g * HEADS + h]
            col = g * GROUP_W + h * HD
            q = _mxu(z_ref[:, col:col + HD])
            kn = _mxu(z_ref[:, ATTN_W + col:ATTN_W + col + HD])
            vn = _mxu(z_ref[:, 2 * ATTN_W + col:2 * ATTN_W + col + HD])
            kc = _mxu(c_ref[pl.ds(h, n_past, stride=2 * HEADS), :])
            vc = _mxu(c_ref[pl.ds(HEADS + h, n_past, stride=2 * HEADS), :])
            sc = lax.dot_general(q, kc, (((1,), (1,)), ((), ())), preferred_element_type=jnp.float32) * scale
            sn = lax.dot_general(q, kn, (((1,), (1,)), ((), ())), preferred_element_type=jnp.float32) * scale
            sc = jnp.where(valid_c, sc - slope * dist_c, NEG_BIG)
            sn = jnp.where(valid_n, sn - slope * dist_n, NEG_BIG)
            m = jnp.maximum(jnp.max(sc, axis=-1, keepdims=True), jnp.max(sn, axis=-1, keepdims=True))
            pc = jnp.exp(sc - m)
            pn = jnp.exp(sn - m)
            l = jnp.sum(pc, axis=-1, keepdims=True) + jnp.sum(pn, axis=-1, keepdims=True)
            o = (_dot(_mxu(pc), vc) + _dot(_mxu(pn), vn)) / l
            o_ref[:, h * HD:(h + 1) * HD] = o
            l_ref[:, h * HD:(h + 1) * HD] = jnp.broadcast_to(m + jnp.log(l), (rows, HD))


def _attn_sample(zs, caches, layer, nreq):
    zc = zs.shape[1]
    z3 = zs.reshape(nreq, SAMPLE_ROWS, zc)
    cache_specs, cache_args = [], []
    for cch in caches:
        depth, nb, n_past = cch.shape[:3]
        assert cch.shape[3:] == (2, HEADS, HD)
        cache_args.append(cch.reshape(depth, nb, n_past * 2 * HEADS, HD))
        cache_specs.append(pl.BlockSpec((None, None, n_past * 2 * HEADS, HD), lambda b: (layer, b, 0, 0)))
    out_spec = pl.BlockSpec((None, SAMPLE_ROWS, GROUP_W), lambda b: (b, 0, 0))
    out_shape = jax.ShapeDtypeStruct((nreq, SAMPLE_ROWS, GROUP_W), jnp.float32)
    outs = pl.pallas_call(
        _attn_sample_kernel,
        grid=(nreq,),
        in_specs=[pl.BlockSpec((None, SAMPLE_ROWS, zc), lambda b: (b, 0, 0))] + cache_specs,
        out_specs=[out_spec] * (2 * N_GROUPS),
        out_shape=[out_shape] * (2 * N_GROUPS),
        compiler_params=_params(("parallel",)),
        name="attn_sample",
    )(z3, *cache_args)
    return [o.reshape(nreq * SAMPLE_ROWS, GROUP_W) for o in outs]


def _ssm_operators(lam_re, lam_im, log_dt, b_re, b_im, c_re, c_im, chunk):
    f32 = jnp.float32
    hi = lax.Precision.HIGHEST
    lam_re, lam_im = lam_re.astype(f32), lam_im.astype(f32)
    dt = jnp.exp(log_dt.astype(f32))[:, None]
    decay = jnp.exp(lam_re * dt)
    a_re = decay * jnp.cos(lam_im * dt)
    a_im = decay * jnp.sin(lam_im * dt)
    den = lam_re * lam_re + lam_im * lam_im
    f_re = ((a_re - 1.0) * lam_re + a_im * lam_im) / den
    f_im = (a_im * lam_re - (a_re - 1.0) * lam_im) / den
    b_re, b_im = b_re.astype(f32), b_im.astype(f32)
    bb_re = f_re[..., None] * b_re - f_im[..., None] * b_im
    bb_im = f_re[..., None] * b_im + f_im[..., None] * b_re
    c_re, c_im = c_re.astype(f32), c_im.astype(f32)

    def power(p):
        p = jnp.asarray(p, f32)[..., None, None]
        mag = jnp.exp(lam_re * dt * p)
        ang = lam_im * dt * p
        return mag * jnp.cos(ang), mag * jnp.sin(ang)

    n_g = lam_re.shape[0]
    ap_re, ap_im = power(jnp.arange(chunk + 1))
    abb_re = ap_re[:chunk, :, :, None] * bb_re - ap_im[:chunk, :, :, None] * bb_im
    abb_im = ap_re[:chunk, :, :, None] * bb_im + ap_im[:chunk, :, :, None] * bb_re
    kern = (jnp.einsum('gcn,tgnd->tgcd', c_re, abb_re, precision=hi)
            - jnp.einsum('gcn,tgnd->tgcd', c_im, abb_im, precision=hi))
    kern = jnp.concatenate([kern, jnp.zeros_like(kern[:1])], axis=0)
    lag = jnp.arange(chunk)[:, None] - jnp.arange(chunk)[None, :]
    m_op = kern[jnp.where(lag >= 0, lag, chunk)]
    m_op = m_op.transpose(2, 0, 3, 1, 4).reshape(n_g, chunk * SSM_C, chunk * SSM_C)
    rev = chunk - 1 - jnp.arange(chunk)
    p_re = abb_re[rev].transpose(1, 2, 0, 3).reshape(n_g, SSM_N, chunk * SSM_C)
    p_im = abb_im[rev].transpose(1, 2, 0, 3).reshape(n_g, SSM_N, chunk * SSM_C)
    p_op = jnp.concatenate([p_re, p_im], axis=1)
    a1_re, a1_im = ap_re[1:], ap_im[1:]
    q_re = c_re[None] * a1_re[:, :, None, :] - c_im[None] * a1_im[:, :, None, :]
    q_im = -(c_re[None] * a1_im[:, :, None, :] + c_im[None] * a1_re[:, :, None, :])
    q_op = jnp.concatenate([q_re, q_im], axis=-1).transpose(1, 0, 2, 3).reshape(n_g, chunk * SSM_C, 2 * SSM_N)
    return m_op, p_op, q_op, power


def _u_proj_kernel(x_ref, w_ref, u_ref, slab):
    n_slab, n_tok, _ = slab.shape
    n_chunk = n_tok // SSM_CHUNK
    groups_per_slab = LANES // SSM_C
    for j2 in range(n_slab // 2):
        zt = _dot(_mxu(x_ref[...]), _mxu(w_ref[:, 2 * j2 * LANES:2 * (j2 + 1) * LANES]))
        slab[2 * j2] = zt[:, :LANES]
        slab[2 * j2 + 1] = zt[:, LANES:]
    for j in range(n_slab):
        for s in range(SSM_CHUNK):
            piece = slab[j, pl.ds(s, n_chunk, stride=SSM_CHUNK), :].T
            for gg in range(groups_per_slab):
                u_ref[j * groups_per_slab + gg, s * SSM_C:(s + 1) * SSM_C, :] = _mxu(piece[gg * SSM_C:(gg + 1) * SSM_C, :])


def _u_proj(x, w, layer, col0, n_g):
    n_tok, k = x.shape
    blk_tok = LANES * SSM_CHUNK
    halves = 2
    cols = n_g * SSM_C // halves
    assert n_tok % blk_tok == 0 and cols % (2 * LANES) == 0 and col0 % cols == 0
    lc = SSM_CHUNK * SSM_C
    return pl.pallas_call(
        _u_proj_kernel,
        grid=(n_tok // blk_tok, halves),
        in_specs=[pl.BlockSpec((blk_tok, k), lambda i, hf: (i, 0)),
                  pl.BlockSpec((None, k, cols), lambda i, hf: (layer, 0, col0 // cols + hf))],
        out_specs=pl.BlockSpec((n_g // halves, lc, LANES), lambda i, hf: (hf, 0, i)),
        out_shape=jax.ShapeDtypeStruct((n_g, lc, n_tok // SSM_CHUNK), MXU_DTYPE),
        scratch_shapes=[pltpu.VMEM((cols // LANES, blk_tok, LANES), jnp.float32)],
        compiler_params=_params(("parallel", "arbitrary")),
        name="u_proj",
    )(x, w)


def _ssm_prompt_kernel(u_ref, m_ref, p_ref, q_ref, pw_ref, y_ref, h_ref, ybuf, *, chunks_per_seq, n_steps):
    n_cols = u_ref.shape[2]
    lane = lax.broadcasted_iota(jnp.int32, (SSM_N, n_cols), 1) % chunks_per_seq
    for gi in range(u_ref.shape[0]):
        u = u_ref[gi]
        xc = _dot(p_ref[gi], u)
        h_re, h_im = xc[:SSM_N], xc[SSM_N:]
        for j in range(n_steps):
            sh = 1 << j
            a_re = pw_ref[gi, :, j:j + 1]
            a_im = pw_ref[gi, :, n_steps + j:n_steps + j + 1]
            keep = lane >= sh
            s_re = jnp.where(keep, pltpu.roll(h_re, sh, 1), 0.0)
            s_im = jnp.where(keep, pltpu.roll(h_im, sh, 1), 0.0)
            h_re, h_im = h_re + a_re * s_re - a_im * s_im, h_im + a_re * s_im + a_im * s_re
        h_ref[gi, 0:SSM_N, :] = h_re
        h_ref[gi, SSM_N:, :] = h_im
        first = lane >= 1
        hp = jnp.concatenate([jnp.where(first, pltpu.roll(h_re, 1, 1), 0.0),
                              jnp.where(first, pltpu.roll(h_im, 1, 1), 0.0)], axis=0)
        ybuf[gi] = _dot(m_ref[gi], u) + _dot(q_ref[gi], _mxu(hp))
    for t in range(SSM_CHUNK):
        rows = jnp.concatenate([ybuf[gi, t * SSM_C:(t + 1) * SSM_C, :] for gi in range(u_ref.shape[0])], axis=0)
        for cb in range(n_cols // LANES):
            y_ref[pl.ds(cb * LANES * SSM_CHUNK + t, LANES, stride=SSM_CHUNK), :] = rows[:, cb * LANES:(cb + 1) * LANES].T


def _ssm_prompt(ut, bsz, seq, ops):
    m_op, p_op, q_op, power = ops
    n_g = m_op.shape[0]
    chunks_per_seq = seq // SSM_CHUNK
    n_cols = bsz * chunks_per_seq
    n_steps = int(math.log2(chunks_per_seq))
    assert (1 << n_steps) == chunks_per_seq and n_cols % LANES == 0
    lc = SSM_CHUNK * SSM_C
    pw_re, pw_im = power(SSM_CHUNK * (2 ** jnp.arange(n_steps)))
    pw = jnp.concatenate([pw_re, pw_im], axis=0).transpose(1, 2, 0)
    gb = LANES // SSM_C
    assert n_g % gb == 0
    y, h = pl.pallas_call(
        functools.partial(_ssm_prompt_kernel, chunks_per_seq=chunks_per_seq, n_steps=n_steps),
        grid=(n_g // gb,),
        in_specs=[pl.BlockSpec((gb, lc, n_cols), lambda i: (i, 0, 0)),
                  pl.BlockSpec((gb, lc, lc), lambda i: (i, 0, 0)),
                  pl.BlockSpec((gb, 2 * SSM_N, lc), lambda i: (i, 0, 0)),
                  pl.BlockSpec((gb, lc, 2 * SSM_N), lambda i: (i, 0, 0)),
                  pl.BlockSpec((gb, SSM_N, 2 * n_steps), lambda i: (i, 0, 0))],
        out_specs=[pl.BlockSpec((bsz * seq, LANES), lambda i: (0, i)),
                   pl.BlockSpec((gb, 2 * SSM_N, n_cols), lambda i: (i, 0, 0))],
        out_shape=[jax.ShapeDtypeStruct((bsz * seq, n_g * SSM_C), jnp.float32),
                   jax.ShapeDtypeStruct((n_g, 2 * SSM_N, n_cols), jnp.float32)],
        scratch_shapes=[pltpu.VMEM((gb, lc, n_cols), jnp.float32)],
        compiler_params=_params(("parallel",)),
        name="ssm_prompt",
    )(ut, _mxu(m_op), _mxu(p_op), _mxu(q_op), pw)
    h_last = h.reshape(n_g, 2, SSM_N, bsz, chunks_per_seq)[..., -1].transpose(3, 0, 2, 1)
    return y, h_last


def _ssm_sample_kernel(u_ref, h0_ref, m_ref, p_ref, q_ref, a_ref, y_ref, h_ref):
    for gi in range(u_ref.shape[0]):
        u = u_ref[gi]
        h0 = h0_ref[gi]
        h0_re, h0_im = h0[:SSM_N], h0[SSM_N:]
        a_re = a_ref[gi, :, 0:1]
        a_im = a_ref[gi, :, 1:2]
        xc = _dot(p_ref[gi], u)
        h_ref[gi, 0:SSM_N, :] = a_re * h0_re - a_im * h0_im + xc[:SSM_N]
        h_ref[gi, SSM_N:, :] = a_re * h0_im + a_im * h0_re + xc[SSM_N:]
        y_ref[gi] = _dot(m_ref[gi], u) + _dot(q_ref[gi], _mxu(h0))


def _ssm_sample(u, h0, nreq, n_new, ops):
    m_op, p_op, q_op, power = ops
    n_g = m_op.shape[0]
    lc = n_new * SSM_C
    ut = u.reshape(nreq, SAMPLE_ROWS, n_g, SSM_C)[:, :n_new].transpose(2, 1, 3, 0).reshape(n_g, lc, nreq)
    ut = _mxu(jnp.pad(ut, ((0, 0), (0, 0), (0, LANES - nreq))))
    h0t = h0.astype(jnp.float32).transpose(1, 3, 2, 0).reshape(n_g, 2 * SSM_N, nreq)
    h0t = jnp.pad(h0t, ((0, 0), (0, 0), (0, LANES - nreq)))
    a_re, a_im = power(n_new)
    a_op = jnp.stack([a_re, a_im], axis=-1)
    gb = SSM_GROUP_BLOCK
    y, h = pl.pallas_call(
        _ssm_sample_kernel,
        grid=(n_g // gb,),
        in_specs=[pl.BlockSpec((gb, lc, LANES), lambda i: (i, 0, 0)),
                  pl.BlockSpec((gb, 2 * SSM_N, LANES), lambda i: (i, 0, 0)),
                  pl.BlockSpec((gb, lc, lc), lambda i: (i, 0, 0)),
                  pl.BlockSpec((gb, 2 * SSM_N, lc), lambda i: (i, 0, 0)),
                  pl.BlockSpec((gb, lc, 2 * SSM_N), lambda i: (i, 0, 0)),
                  pl.BlockSpec((gb, SSM_N, 2), lambda i: (i, 0, 0))],
        out_specs=[pl.BlockSpec((gb, lc, LANES), lambda i: (i, 0, 0)),
                   pl.BlockSpec((gb, 2 * SSM_N, LANES), lambda i: (i, 0, 0))],
        out_shape=[jax.ShapeDtypeStruct((n_g, lc, LANES), jnp.float32),
                   jax.ShapeDtypeStruct((n_g, 2 * SSM_N, LANES), jnp.float32)],
        compiler_params=_params(("parallel",)),
        name="ssm_sample",
    )(ut, h0t, _mxu(m_op), _mxu(p_op), _mxu(q_op), a_op)
    y = y[:, :, :nreq].reshape(n_g, n_new, SSM_C, nreq).transpose(3, 1, 0, 2).reshape(nreq, n_new, n_g * SSM_C)
    y = jnp.pad(y, ((0, 0), (0, SAMPLE_ROWS - n_new), (0, 0))).reshape(nreq * SAMPLE_ROWS, n_g * SSM_C)
    h_new = h[:, :, :nreq].reshape(n_g, 2, SSM_N, nreq).transpose(3, 0, 2, 1)
    return y, h_new


def _mix_kernel(o0_ref, l0_ref, o1_ref, l1_ref, o2_ref, l2_ref, ys_ref, u_ref, ga_ref, gs_ref, dsk_ref, wglu_ref,
                bglu_ref, wap_ref, wsp_ref, o_ref, attn_buf):
    outs = (o0_ref, o1_ref, o2_ref)
    for j in range(HEADS):
        sl = slice(j * HD, (j + 1) * HD)
        lses = [l[:, sl] for l in (l0_ref, l1_ref, l2_ref)]
        top = jnp.maximum(jnp.maximum(lses[0], lses[1]), lses[2])
        ws = [jnp.exp(l - top) for l in lses]
        num = ws[0] * outs[0][:, sl] + ws[1] * outs[1][:, sl] + ws[2] * outs[2][:, sl]
        attn_buf[:, sl] = _mxu(num / (ws[0] + ws[1] + ws[2]))
    y = jax.nn.gelu(ys_ref[...] + dsk_ref[...] * u_ref[...])
    ssm = y * jax.nn.sigmoid(_dot(_mxu(y), wglu_ref[...]) + bglu_ref[...])
    merged = (jax.nn.sigmoid(ga_ref[...]) * _dot(attn_buf[...], wap_ref[...])
              + jax.nn.sigmoid(gs_ref[...]) * _dot(_mxu(ssm), wsp_ref[...]))
    o_ref[...] = _mxu(merged)


def _mix(parts, ys, z, d_skip, w_glu, b_glu, w_ap, w_sp, tm):
    m = ys.shape[0]
    ssm_w = ys.shape[1]
    d_model = w_ap.shape[1]
    u_off, ga_off = 3 * ATTN_W, 3 * ATTN_W + ssm_w
    gs_off = ga_off + d_model
    assert u_off % ssm_w == 0 and ga_off % d_model == 0 and m % tm == 0
    row = lambda w: pl.BlockSpec((tm, w), lambda i: (i, 0))
    const = lambda a: pl.BlockSpec(a.shape, lambda i: (0,) * a.ndim)
    d_skip, b_glu = d_skip.reshape(1, -1), b_glu.reshape(1, -1)
    return pl.pallas_call(
        _mix_kernel,
        grid=(m // tm,),
        in_specs=[row(GROUP_W)] * (2 * N_GROUPS) + [row(ssm_w),
                  pl.BlockSpec((tm, ssm_w), lambda i: (i, u_off // ssm_w)),
                  pl.BlockSpec((tm, d_model), lambda i: (i, ga_off // d_model)),
                  pl.BlockSpec((tm, d_model), lambda i: (i, gs_off // d_model)),
                  const(d_skip), const(w_glu), const(b_glu), const(w_ap), const(w_sp)],
        out_specs=row(d_model),
        out_shape=jax.ShapeDtypeStruct((m, d_model), MXU_DTYPE),
        scratch_shapes=[pltpu.VMEM((tm, GROUP_W), MXU_DTYPE)],
        compiler_params=_params(("parallel",)),
        name="mix",
    )(*parts, ys, z, z, z, d_skip, w_glu, b_glu, w_ap, w_sp)


def _layer_norm(h, g, b):
    mu = jnp.mean(h, axis=-1, keepdims=True)
    d = h - mu
    var = jnp.mean(d * d, axis=-1, keepdims=True)
    return d * lax.rsqrt(var + LN_EPS) * g + b


def _split_hi_lo(a):
    hi = _mxu(a)
    return hi, _mxu(a - hi.astype(jnp.float32))


def _out_kernel(m_ref, x_ref, wo_ref, g_ref, b_ref, rw_ref, rb_ref, x1_ref, lg_ref, *, alpha):
    h = alpha * x_ref[...] + _dot(m_ref[...], wo_ref[...])
    x1 = _layer_norm(h, g_ref[...], b_ref[...])
    x1_ref[...] = x1
    hi, lo = _split_hi_lo(x1)
    both = _dot(hi, rw_ref[...])
    lg_ref[...] = both[:, :LANES] + both[:, LANES:] + _dot(lo, rw_ref[:, :LANES]) + rb_ref[...]


def _out_proj(merged, x, w_o, g, b, rw, rb, alpha, tm):
    m, d_model = x.shape
    assert m % tm == 0
    row = lambda w: pl.BlockSpec((tm, w), lambda i: (i, 0))
    const = lambda a: pl.BlockSpec(a.shape, lambda i: (0,) * a.ndim)
    g, b = g.reshape(1, -1), b.reshape(1, -1)
    return pl.pallas_call(
        functools.partial(_out_kernel, alpha=alpha),
        grid=(m // tm,),
        in_specs=[row(d_model), row(d_model), const(w_o), const(g), const(b), const(rw), const(rb)],
        out_specs=[row(d_model), row(LANES)],
        out_shape=[jax.ShapeDtypeStruct((m, d_model), jnp.float32),
                   jax.ShapeDtypeStruct((m, LANES), jnp.float32)],
        compiler_params=_params(("parallel",)),
        name="out_proj",
    )(merged, x, w_o, g, b, rw, rb)


def _moe_kernel(ce_ref, cr_ref, cn_ref, ct_ref, cm_ref, tot_ref, tok_ref, x_hbm, wg_ref, wl_ref, wd_ref, bg_ref,
                bl_ref, bd_ref, y_hbm, xbuf, acc, wgb, wlb, wdb, sem_in, sem_out, *, n_ff_tiles, n_sub_total):
    c = pl.program_id(0)
    j = pl.program_id(1)
    nsb = cn_ref[c]
    row0 = cr_ref[c]
    tok0 = ct_ref[c]
    last_real = cm_ref[c] - 1

    def sub_rows(s):
        return pl.ds(pl.multiple_of(s * MOE_SUB, MOE_SUB), MOE_SUB)

    def x_copy(i):
        tok = tok_ref[tok0 + jnp.minimum(i, last_real)]
        return pltpu.make_async_copy(x_hbm.at[pl.ds(tok, 1)], xbuf.at[pl.ds(i, 1)], sem_in)

    def y_copy(s, first_sub):
        return pltpu.make_async_copy(acc.at[sub_rows(s)], y_hbm.at[sub_rows(first_sub + s)], sem_out)

    def for_subs(n, fn):
        def body(s, carry):
            fn(s)
            return carry
        lax.fori_loop(0, n, body, 0)

    def for_rows(n_sub, fn):
        def body(g, carry):
            for k in range(MOE_ROW_UNROLL):
                fn(g * MOE_ROW_UNROLL + k)
            return carry
        lax.fori_loop(0, n_sub * (MOE_SUB // MOE_ROW_UNROLL), body, 0)

    @pl.when((j == 0) & (nsb > 0))
    def _():
        for_rows(nsb, lambda i: x_copy(i).start())

    @pl.when((j == 0) & (c > 0))
    def _():
        prev = jnp.maximum(c - 1, 0)
        for_subs(cn_ref[prev], lambda s: y_copy(s, cr_ref[prev]).wait())

    @pl.when((j == 0) & (nsb > 0))
    def _():
        def init(s):
            acc[sub_rows(s), :] = jnp.broadcast_to(bd_ref[...], (MOE_SUB, acc.shape[1]))

        for_subs(nsb, init)
        for_rows(nsb, lambda i: x_copy(i).wait())

    @pl.when(nsb > 0)
    def _():
        wgb[...] = _mxu(wg_ref[...])
        wlb[...] = _mxu(wl_ref[...])
        wdb[...] = _mxu(wd_ref[...])

        def sub(s):
            rows = sub_rows(s)
            xs = _mxu(xbuf[rows, :])
            glu = jnp.minimum(_dot(xs, wgb[...]) + bg_ref[...], SWIGLU_LIMIT)
            lin = jnp.clip(_dot(xs, wlb[...]) + bl_ref[...], -SWIGLU_LIMIT, SWIGLU_LIMIT)
            act = glu * jax.nn.sigmoid(SWIGLU_ALPHA * glu) * (lin + 1.0)
            acc[rows, :] += _dot(_mxu(act), wdb[...])

        done = 0
        for width in MOE_GROUP_WIDTHS:
            def group(p, base=done, width=width):
                for k in range(width):
                    sub(base + width * p + k)

            trips = (nsb - done) // width
            for_subs(trips, group)
            done = done + trips * width

    @pl.when((j == n_ff_tiles - 1) & (nsb > 0))
    def _():
        for_subs(nsb, lambda s: y_copy(s, row0).start())

    @pl.when((c == pl.num_programs(0) - 1) & (j == n_ff_tiles - 1))
    def _():
        for_subs(nsb, lambda s: y_copy(s, row0).wait())
        first = tot_ref[0] // MOE_SUB
        acc[0:MOE_SUB, :] = jnp.zeros((MOE_SUB, acc.shape[1]), acc.dtype)

        def z_copy(s):
            return pltpu.make_async_copy(acc.at[pl.ds(0, MOE_SUB)], y_hbm.at[sub_rows(first + s)], sem_out)

        for_subs(n_sub_total - first, lambda s: z_copy(s).start())
        for_subs(n_sub_total - first, lambda s: z_copy(s).wait())


def _moe_experts(x, route, w_gu, b_gu, w_dn, b_dn, layer):
    chunk_expert = route["chunk_expert"]
    n_slots = route["n_slots"]
    d_model = x.shape[1]
    d_ff = w_dn.shape[2]
    tf = MOE_FF_TILE
    n_ff_tiles = d_ff // tf
    n_chunks = chunk_expert.shape[0]
    rows = MOE_SUB * MOE_CHUNK_SUBS
    n_exp = w_gu.shape[1]
    b_gu4 = b_gu.reshape(b_gu.shape[0], n_exp, 1, 2 * d_ff)
    b_dn4 = b_dn.reshape(b_dn.shape[0], n_exp, 1, d_model)

    def ff(j, c, cn):
        return jnp.where(cn[c] > 0, j, n_ff_tiles - 1)

    def wspec(shape, index):
        return pl.BlockSpec((None, None) + shape,
                            lambda c, j, ce, cr, cn, *_: (layer, ce[c]) + index(ff(j, c, cn)))

    grid_spec = pltpu.PrefetchScalarGridSpec(
        num_scalar_prefetch=7,
        grid=(n_chunks, n_ff_tiles),
        in_specs=[
            pl.BlockSpec(memory_space=pl.ANY),
            wspec((d_model, tf), lambda t: (0, t)),
            wspec((d_model, tf), lambda t: (0, n_ff_tiles + t)),
            wspec((tf, d_model), lambda t: (t, 0)),
            wspec((1, tf), lambda t: (0, t)),
            wspec((1, tf), lambda t: (0, n_ff_tiles + t)),
            wspec((1, d_model), lambda t: (0, 0)),
        ],
        out_specs=pl.BlockSpec(memory_space=pl.ANY),
        scratch_shapes=[
            pltpu.VMEM((rows, d_model), jnp.float32),
            pltpu.VMEM((rows, d_model), jnp.float32),
            pltpu.VMEM((d_model, tf), MXU_DTYPE),
            pltpu.VMEM((d_model, tf), MXU_DTYPE),
            pltpu.VMEM((tf, d_model), MXU_DTYPE),
            pltpu.SemaphoreType.DMA(()),
            pltpu.SemaphoreType.DMA(()),
        ],
    )
    return pl.pallas_call(
        functools.partial(_moe_kernel, n_ff_tiles=n_ff_tiles, n_sub_total=n_slots // MOE_SUB),
        grid_spec=grid_spec,
        out_shape=jax.ShapeDtypeStruct((n_slots, d_model), jnp.float32),
        compiler_params=_params(("arbitrary", "arbitrary"), MOE_VMEM_LIMIT_BYTES),
        name="moe_experts",
    )(chunk_expert, route["chunk_row"], route["chunk_subs"], route["chunk_tok"], route["chunk_real"],
      route["total_rows"], route["tok_sorted"], x, w_gu, w_gu, w_dn, b_gu4, b_gu4, b_dn4)


def _route(logits, n_exp):
    n_tok = logits.shape[0]
    top_val, top_idx = lax.top_k(logits[:, :n_exp], TOP_K)
    gate = jax.nn.softmax(top_val, axis=-1)
    n_assign = n_tok * TOP_K
    n_slots = -(-(n_assign + n_exp * (MOE_SUB - 1)) // MOE_SUB) * MOE_SUB
    flat_e = top_idx.reshape(-1)
    onehot = (flat_e[:, None] == jnp.arange(n_exp, dtype=flat_e.dtype)[None, :]).astype(jnp.int32)
    before = jnp.cumsum(onehot, axis=0) - onehot
    rank = jnp.take_along_axis(before, flat_e[:, None], axis=1)[:, 0]
    counts = onehot.sum(axis=0)
    padded = (counts + MOE_SUB - 1) // MOE_SUB * MOE_SUB
    pad_end = jnp.cumsum(padded)
    pad_start = pad_end - padded
    dest = (pad_start[flat_e] + rank).astype(jnp.int32)
    tok_sorted = (jnp.argsort(flat_e, stable=True) // TOP_K).astype(jnp.int32)
    grp_start = jnp.cumsum(counts) - counts
    rows = MOE_SUB * MOE_CHUNK_SUBS
    n_chunks = n_slots // rows + n_exp + 1
    per_exp = (padded + rows - 1) // rows
    chunk_end = jnp.cumsum(per_exp)
    cid = jnp.arange(n_chunks)
    exp_of = jnp.minimum(jnp.searchsorted(chunk_end, cid, side='right'), n_exp - 1)
    within = cid - (chunk_end - per_exp)[exp_of]
    live = cid < chunk_end[-1]
    row0 = pad_start[exp_of] + within * rows
    subs = jnp.clip(padded[exp_of] - within * rows, 0, rows) // MOE_SUB
    last_exp = exp_of[jnp.maximum(chunk_end[-1] - 1, 0)]
    as_i32 = lambda a: a.astype(jnp.int32)
    route = dict(
        n_slots=n_slots,
        chunk_expert=as_i32(jnp.where(live, exp_of, last_exp)),
        chunk_row=as_i32(jnp.where(live, row0, 0) // MOE_SUB),
        chunk_subs=as_i32(jnp.where(live, subs, 0)),
        chunk_tok=as_i32(jnp.where(live, grp_start[exp_of] + within * rows, 0)),
        chunk_real=as_i32(jnp.where(live, jnp.clip(counts[exp_of] - within * rows, 1, rows), 1)),
        total_rows=as_i32(pad_end[-1:]),
        tok_sorted=tok_sorted,
    )
    return gate, dest, route


def _combine_kernel(x_ref, y0_ref, y1_ref, y2_ref, y3_ref, gate_ref, g_ref, b_ref, o_ref, ob_ref, *, alpha):
    h = alpha * x_ref[...]
    for k, y_ref in enumerate((y0_ref, y1_ref, y2_ref, y3_ref)):
        h = h + gate_ref[:, k:k + 1] * y_ref[...]
    x2 = _layer_norm(h, g_ref[...], b_ref[...])
    o_ref[...] = x2
    ob_ref[...] = _mxu(x2)


def _combine(x1, yb_k, gate, g, b, alpha, tm):
    m, d_model = x1.shape
    assert m % tm == 0 and len(yb_k) == TOP_K
    row = lambda w: pl.BlockSpec((tm, w), lambda i: (i, 0))
    const = lambda a: pl.BlockSpec(a.shape, lambda i: (0,) * a.ndim)
    g, b = g.reshape(1, -1), b.reshape(1, -1)
    gate = jnp.pad(gate, ((0, 0), (0, LANES - TOP_K)))
    return pl.pallas_call(
        functools.partial(_combine_kernel, alpha=alpha),
        grid=(m // tm,),
        in_specs=[row(d_model)] * (1 + TOP_K) + [row(LANES), const(g), const(b)],
        out_specs=[row(d_model), row(d_model)],
        out_shape=[jax.ShapeDtypeStruct((m, d_model), jnp.float32),
                   jax.ShapeDtypeStruct((m, d_model), MXU_DTYPE)],
        compiler_params=_params(("parallel",)),
        name="combine",
    )(x1, *yb_k, gate, g, b)


def _forward(x_prompt, x_sample, cache_kv_w128, cache_kv_w512, cache_kv_w2048, state_ssm, w_in, w_attn_proj,
             ssm_lambda_re, ssm_lambda_im, ssm_log_dt, ssm_b_re, ssm_b_im, ssm_c_re, ssm_c_im, ssm_d, w_glu,
             b_glu, w_ssm_proj, w_o, ln1_g, ln1_b, router_w, router_b, w_gate_up, b_gate_up, w_down, b_down,
             ln2_g, ln2_b):
    caches = (cache_kv_w128, cache_kv_w512, cache_kv_w2048)
    bsz, seq, d_model = x_prompt.shape
    nreq, n_new, _ = x_sample.shape
    depth = w_in.shape[0]
    n_exp = router_w.shape[2]
    alpha = (2 * depth) ** 0.25
    n_p = bsz * seq
    n_s = nreq * SAMPLE_ROWS
    assert n_new <= SAMPLE_ROWS

    xp = x_prompt.reshape(n_p, d_model)
    xs = jnp.pad(x_sample, ((0, 0), (0, SAMPLE_ROWS - n_new), (0, 0))).reshape(n_s, d_model)
    xpb, xsb = _mxu(xp), _mxu(xs)
    kv_p = [[] for _ in ATTN_PATTERNS]
    kv_s = [[] for _ in ATTN_PATTERNS]
    h_p, h_s = [], []
    tm_in = 1024 if n_p % 1024 == 0 else 256

    for layer in range(depth):
        w_glu_l, w_ap_l, w_sp_l, w_o_l = (_mxu(w[layer]) for w in (w_glu, w_attn_proj, w_ssm_proj, w_o))
        rw_l = jnp.concatenate(_split_hi_lo(jnp.pad(router_w[layer], ((0, 0), (0, LANES - n_exp)))), axis=1)
        rb_l = jnp.pad(router_b[layer], (0, LANES - n_exp)).reshape(1, LANES)
        lam = (ssm_lambda_re[layer], ssm_lambda_im[layer], ssm_log_dt[layer], ssm_b_re[layer], ssm_b_im[layer],
               ssm_c_re[layer], ssm_c_im[layer])
        ssm_w = ssm_d.shape[1]
        u_lo = 3 * ATTN_W

        zp = _matmul(xpb, w_in, layer, tm_in, 1024)
        z3 = zp.reshape(bsz, seq, -1)
        parts_p = [a for g in range(N_GROUPS) for a in _attn_prompt(z3, g)]
        ut = _u_proj(xpb, w_in, layer, u_lo, ssm_w // SSM_C)
        ys_p, h_last = _ssm_prompt(ut, bsz, seq, _ssm_operators(*lam, SSM_CHUNK))
        h_p.append(h_last)
        for g, (window, _) in enumerate(ATTN_PATTERNS):
            keep = min(window, seq)
            k_g = z3[:, seq - keep:, ATTN_W + g * GROUP_W:ATTN_W + (g + 1) * GROUP_W]
            v_g = z3[:, seq - keep:, 2 * ATTN_W + g * GROUP_W:2 * ATTN_W + (g + 1) * GROUP_W]
            kv_p[g].append(jnp.stack([k_g, v_g], axis=2).reshape(bsz, keep, 2, HEADS, HD))
        mg_p = _mix(parts_p, ys_p, zp, ssm_d[layer], w_glu_l, b_glu[layer], w_ap_l, w_sp_l, 128)
        x1_p, lg_p = _out_proj(mg_p, xp, w_o_l, ln1_g[layer], ln1_b[layer], rw_l, rb_l, alpha, 256)

        zs = _matmul(xsb, w_in, layer, n_s, 1024)
        parts_s = _attn_sample(zs, caches, layer, nreq)
        ys_s, h_new = _ssm_sample(zs[:, u_lo:u_lo + ssm_w], state_ssm[layer], nreq, n_new,
                                  _ssm_operators(*lam, n_new))
        h_s.append(h_new)
        zs3 = zs.reshape(nreq, SAMPLE_ROWS, -1)[:, :n_new]
        for g in range(N_GROUPS):
            k_g = zs3[:, :, ATTN_W + g * GROUP_W:ATTN_W + (g + 1) * GROUP_W]
            v_g = zs3[:, :, 2 * ATTN_W + g * GROUP_W:2 * ATTN_W + (g + 1) * GROUP_W]
            kv_s[g].append(jnp.stack([k_g, v_g], axis=2).reshape(nreq, n_new, 2, HEADS, HD))
        mg_s = _mix(parts_s, ys_s, zs, ssm_d[layer], w_glu_l, b_glu[layer], w_ap_l, w_sp_l, n_s)
        x1_s, lg_s = _out_proj(mg_s, xs, w_o_l, ln1_g[layer], ln1_b[layer], rw_l, rb_l, alpha, n_s)

        logits = jnp.concatenate([lg_p, lg_s], axis=0)
        gate, dest, route = _route(logits, n_exp)
        yb = _moe_experts(jnp.concatenate([x1_p, x1_s], axis=0), route, w_gate_up, b_gate_up, w_down, b_down, layer)
        dest = dest.reshape(n_p + n_s, TOP_K)
        xp, xpb = _combine(x1_p, [yb[dest[:n_p, k]] for k in range(TOP_K)], gate[:n_p],
                           ln2_g[layer], ln2_b[layer], alpha, 256)
        xs, xsb = _combine(x1_s, [yb[dest[n_p:, k]] for k in range(TOP_K)], gate[n_p:],
                           ln2_g[layer], ln2_b[layer], alpha, n_s)

    y_prompt = xp.reshape(bsz, seq, d_model)
    y_sample = xs.reshape(nreq, SAMPLE_ROWS, d_model)[:, :n_new]
    kv_s_out = [jnp.concatenate([c.astype(rows[0].dtype), jnp.stack(rows)], axis=2)[:, :, -c.shape[2]:]
                for rows, c in zip(kv_s, caches)]
    return (y_prompt, y_sample,
            jnp.stack(kv_p[0]), jnp.stack(kv_p[1]), jnp.stack(kv_p[2]), jnp.stack(h_p),
            kv_s_out[0], kv_s_out[1], kv_s_out[2], jnp.stack(h_s))


_forward_jit = jax.jit(_forward)


def kernel(x_prompt, x_sample, cache_kv_w128, cache_kv_w512, cache_kv_w2048, state_ssm, w_in, w_attn_proj,
           ssm_lambda_re, ssm_lambda_im, ssm_log_dt, ssm_b_re, ssm_b_im, ssm_c_re, ssm_c_im, ssm_d, w_glu, b_glu,
           w_ssm_proj, w_o, ln1_g, ln1_b, router_w, router_b, w_gate_up, b_gate_up, w_down, b_down, ln2_g, ln2_b):
    return _forward_jit(x_prompt, x_sample, cache_kv_w128, cache_kv_w512, cache_kv_w2048, state_ssm, w_in,
                        w_attn_proj, ssm_lambda_re, ssm_lambda_im, ssm_log_dt, ssm_b_re, ssm_b_im, ssm_c_re,
                        ssm_c_im, ssm_d, w_glu, b_glu, w_ssm_proj, w_o, ln1_g, ln1_b, router_w, router_b,
                        w_gate_up, b_gate_up, w_down, b_down, ln2_g, ln2_b)
```
